```python
import jax, jax.numpy as jnp
from jax import lax
import numpy as np

D_MODEL = 1024
BATCH = 1
SEQ = 16384
DEPTH = 1

D_CONV = D_MODEL
CONV_W = 3
N_HEADS = 8
HEAD_DIM = 128
D_ATTN = N_HEADS * HEAD_DIM
MOBA_BLOCK = 256
MOBA_TOPK = 3
Q_CHUNK = 64
D_FF = 2816
EPS = 1e-6
D_IN = 3 * D_CONV + 3 * D_ATTN + 2 * D_MODEL

kernel_name = "hybrid_conv_moba_gated_block"


def rmsnorm(x, g):
    x32 = x.astype(jnp.float32)
    y = x32 * lax.rsqrt(jnp.mean(x32 * x32, axis=-1, keepdims=True) + EPS)
    return y.astype(x.dtype) * g


def causal_dwconv(x, w):
    s = x.shape[1]
    xp = jnp.pad(x, ((0, 0), (CONV_W - 1, 0), (0, 0)))
    y = w[0] * xp[:, 0:s]
    for j in range(1, CONV_W):
        y = y + w[j] * xp[:, j:j + s]
    return y


def moba_attention(q, k, v):
    b, h, s, hd = q.shape
    nb = -(-s // MOBA_BLOCK)
    s_pad = nb * MOBA_BLOCK
    pad = ((0, 0), (0, 0), (0, s_pad - s), (0, 0))
    kp = jnp.pad(k, pad)
    vp = jnp.pad(v, pad)
    kb = kp.reshape(b, h, nb, MOBA_BLOCK, hd)
    vb = vp.reshape(b, h, nb, MOBA_BLOCK, hd)
    kbar = jnp.mean(kb.astype(jnp.float32), axis=3).astype(k.dtype)
    n_sel = min(MOBA_TOPK, nb)
    scale = hd ** -0.5
    bidx = jnp.arange(b)[:, None, None, None]
    hidx = jnp.arange(h)[None, :, None, None]

    def one_chunk(c):
        q0 = c * Q_CHUNK
        qc = lax.dynamic_slice_in_dim(q, q0, Q_CHUNK, axis=2)
        qpos = q0 + jnp.arange(Q_CHUNK)
        cur = q0 // MOBA_BLOCK
        gate = jnp.einsum('bhqd,bhnd->bhqn', qc, kbar).astype(jnp.float32)
        gate = jnp.where(jnp.arange(nb) < cur, gate, -jnp.inf)
        _, idx = lax.top_k(gate, n_sel)
        valid = jnp.arange(n_sel) < cur
        kg = kb[bidx, hidx, idx]
        vg = vb[bidx, hidx, idx]
        s_sel = jnp.einsum('bhqd,bhqnld->bhqnl', qc, kg).astype(jnp.float32) * scale
        s_sel = jnp.where(valid[:, None], s_sel, -jnp.inf).reshape(b, h, Q_CHUNK, n_sel * MOBA_BLOCK)
        k_own = lax.dynamic_slice_in_dim(kp, cur * MOBA_BLOCK, MOBA_BLOCK, axis=2)
        v_own = lax.dynamic_slice_in_dim(vp, cur * MOBA_BLOCK, MOBA_BLOCK, axis=2)
        kpos = cur * MOBA_BLOCK + jnp.arange(MOBA_BLOCK)
        s_own = jnp.einsum('bhqd,bhkd->bhqk', qc, k_own).astype(jnp.float32) * scale
        s_own = jnp.where(kpos[None, :] <= qpos[:, None], s_own, -jnp.inf)
        p = jax.nn.softmax(jnp.concatenate([s_sel, s_own], axis=-1), axis=-1)
        p_sel = p[..., :n_sel * MOBA_BLOCK].reshape(b, h, Q_CHUNK, n_sel, MOBA_BLOCK).astype(v.dtype)
        p_own = p[..., n_sel * MOBA_BLOCK:].astype(v.dtype)
        return (jnp.einsum('bhqnl,bhqnld->bhqd', p_sel, vg)
                + jnp.einsum('bhqk,bhkd->bhqd', p_own, v_own))

    out = lax.map(one_chunk, jnp.arange(s // Q_CHUNK))
    return out.transpose(1, 2, 0, 3, 4).reshape(b, h, s, hd)


def setup_inputs(seed: int = 0) -> dict:
    key = jax.random.key(seed)
    ks = jax.random.split(key, 16)
    nrm = lambda k, shape, fan_in: jax.random.normal(k, shape, jnp.float32) * (fan_in ** -0.5)
    return {
        "x": jax.random.normal(ks[0], (BATCH, SEQ, D_MODEL), jnp.float32),
        "norm_mix": 1.0 + 0.02 * jax.random.normal(ks[1], (D_MODEL,), jnp.float32),
        "w_in": nrm(ks[2], (D_MODEL, D_IN), D_MODEL),
        "conv_a_w": nrm(ks[3], (CONV_W, D_CONV), CONV_W),
        "q_norm": 1.0 + 0.02 * jax.random.normal(ks[4], (HEAD_DIM,), jnp.float32),
        "k_norm": 1.0 + 0.02 * jax.random.normal(ks[5], (HEAD_DIM,), jnp.float32),
        "w_out_a": nrm(ks[6], (D_CONV, D_MODEL), D_CONV),
        "w_out_b": nrm(ks[7], (D_ATTN, D_MODEL), D_ATTN),
        "w_o": nrm(ks[8], (D_MODEL, D_MODEL), D_MODEL),
        "norm_ffn": 1.0 + 0.02 * jax.random.normal(ks[9], (D_MODEL,), jnp.float32),
        "w_up": nrm(ks[10], (D_MODEL, 2 * D_FF), D_MODEL),
        "ffn_conv_w": nrm(ks[11], (CONV_W, 2 * D_FF), CONV_W),
        "ffn_conv_b": 0.02 * jax.random.normal(ks[12], (2 * D_FF,), jnp.float32),
        "w_down": nrm(ks[13], (D_FF, D_MODEL), D_FF),
    }


def reference(x, norm_mix, w_in, conv_a_w, q_norm, k_norm, w_out_a, w_out_b, w_o,
              norm_ffn, w_up, ffn_conv_w, ffn_conv_b, w_down):
    b, s, _ = x.shape
    for _layer in range(DEPTH):
        xn = rmsnorm(x, norm_mix)
        z = xn @ w_in
        offs = np.cumsum([0, D_CONV, D_CONV, D_CONV, D_ATTN, D_ATTN, D_ATTN, D_MODEL, D_MODEL])
        bg, cg, xa, q, k, v, ga, gb = [z[..., offs[i]:offs[i + 1]] for i in range(8)]
        branch_a = (bg * causal_dwconv(cg * xa, conv_a_w)) @ w_out_a
        heads = lambda t: t.reshape(b, s, N_HEADS, HEAD_DIM).transpose(0, 2, 1, 3)
        qh = rmsnorm(heads(q), q_norm)
        kh = rmsnorm(heads(k), k_norm)
        o = moba_attention(qh, kh, heads(v))
        branch_b = o.transpose(0, 2, 1, 3).reshape(b, s, D_ATTN) @ w_out_b
        merged = jax.nn.sigmoid(ga) * branch_a + jax.nn.sigmoid(gb) * branch_b
        x = x + merged @ w_o
        hn = rmsnorm(x, norm_ffn)
        u = causal_dwconv(hn @ w_up, ffn_conv_w) + ffn_conv_b
        gate, val = u[..., :D_FF], u[..., D_FF:]
        x = x + (jax.nn.silu(gate) * val) @ w_down
    return x
```

```python
import functools

import jax
import jax.numpy as jnp
from jax import lax
from jax.experimental import pallas as pl
from jax.experimental.pallas import tpu as pltpu

D_MODEL = 1024
N_HEADS = 8
HEAD_DIM = 128
MOBA_BLOCK = 256
MOBA_TOPK = 3
D_FF = 2816
EPS = 1e-6
CONV_HALO_ROWS = 8
NEG_BIG = -1e30

PROJ_ROWS = 256
POST_ROWS = 512
FFN_CHUNK = 256
VMEM_LIMIT = 56 * 1024 * 1024

_BF = jnp.bfloat16
_F32 = jnp.float32


def _dot(a, b):
    return jnp.dot(a, b, preferred_element_type=_F32)


def _dot_nt(a, b):
    return lax.dot_general(a, b, (((1,), (1,)), ((), ())), preferred_element_type=_F32)


def _rms(x):
    return x * lax.rsqrt(jnp.mean(x * x, axis=-1, keepdims=True) + EPS)


def _causal_conv3(c, prev, w):
    row = lax.broadcasted_iota(jnp.int32, c.shape, 0)
    p1 = prev[CONV_HALO_ROWS - 1:CONV_HALO_ROWS, :]
    p2 = prev[CONV_HALO_ROWS - 2:CONV_HALO_ROWS - 1, :]
    c1 = jnp.where(row == 0, p1, pltpu.roll(c, 1, 0))
    c2 = jnp.where(row == 0, p2, jnp.where(row == 1, p1, pltpu.roll(c, 2, 0)))
    return w[0:1, :] * c2 + w[1:2, :] * c1 + w[2:3, :] * c


def _proj_kernel(x_ref, nm_ref, wc_ref, wqk_ref, wvt_ref, wg_ref, cw_ref, qn_ref, kn_ref, woa_ref,
                 sa_ref, sgb_ref, q_ref, k_ref, vt_ref, kbar_ref, halo_ref):
    i = pl.program_id(0)

    @pl.when(i == 0)
    def _():
        halo_ref[...] = jnp.zeros_like(halo_ref)

    xb = (_rms(x_ref[...]) * nm_ref[...]).astype(_BF)

    zc = _dot(xb, wc_ref[...])
    bg = zc[:, :D_MODEL]
    c = zc[:, D_MODEL:2 * D_MODEL] * zc[:, 2 * D_MODEL:]
    y = _causal_conv3(c, halo_ref[...], cw_ref[...])
    halo_ref[...] = c[PROJ_ROWS - CONV_HALO_ROWS:, :]
    branch_a = _dot((bg * y).astype(_BF), woa_ref[...])

    zg = _dot(xb, wg_ref[...])
    sa_ref[...] = jax.nn.sigmoid(zg[:, :D_MODEL]) * branch_a
    sgb_ref[...] = jax.nn.sigmoid(zg[:, D_MODEL:])

    zqk = _dot(xb, wqk_ref[...])
    for h in range(N_HEADS):
        lo = h * HEAD_DIM
        qh = _rms(zqk[:, lo:lo + HEAD_DIM]) * qn_ref[...]
        kh = _rms(zqk[:, D_MODEL + lo:D_MODEL + lo + HEAD_DIM]) * kn_ref[...]
        q_ref[:, lo:lo + HEAD_DIM] = qh.astype(_BF)
        k_ref[:, lo:lo + HEAD_DIM] = kh.astype(_BF)
        kbar_ref[0, :, lo:lo + HEAD_DIM] = jnp.mean(kh, axis=0, keepdims=True)

    vt = _dot_nt(wvt_ref[...], xb)
    for h in range(N_HEADS):
        vt_ref[h, 0] = vt[h * HEAD_DIM:(h + 1) * HEAD_DIM, :].astype(_BF)


def _proj(x, nm, wc, wqk, wvt, wg, cw, qn, kn, woa):
    s = x.shape[0]
    nb = s // PROJ_ROWS
    const = lambda shape: pl.BlockSpec(shape, lambda i: (0,) * len(shape), pipeline_mode=pl.Buffered(1))
    rows = lambda w: pl.BlockSpec((PROJ_ROWS, w), lambda i: (i, 0))
    return pl.pallas_call(
        _proj_kernel,
        grid=(nb,),
        in_specs=[
            rows(D_MODEL),
            const((1, D_MODEL)),
            const((D_MODEL, 3 * D_MODEL)),
            const((D_MODEL, 2 * D_MODEL)),
            const((D_MODEL, D_MODEL)),
            const((D_MODEL, 2 * D_MODEL)),
            const((3, D_MODEL)),
            const((1, HEAD_DIM)),
            const((1, HEAD_DIM)),
            const((D_MODEL, D_MODEL)),
        ],
        out_specs=[
            rows(D_MODEL),
            rows(D_MODEL),
            rows(D_MODEL),
            rows(D_MODEL),
            pl.BlockSpec((N_HEADS, 1, HEAD_DIM, MOBA_BLOCK), lambda i: (0, i, 0, 0)),
            pl.BlockSpec((1, 1, D_MODEL), lambda i: (i, 0, 0)),
        ],
        out_shape=[
            jax.ShapeDtypeStruct((s, D_MODEL), _F32),
            jax.ShapeDtypeStruct((s, D_MODEL), _F32),
            jax.ShapeDtypeStruct((s, D_MODEL), _BF),
            jax.ShapeDtypeStruct((s, D_MODEL), _BF),
            jax.ShapeDtypeStruct((N_HEADS, nb, HEAD_DIM, MOBA_BLOCK), _BF),
            jax.ShapeDtypeStruct((nb, 1, D_MODEL), _F32),
        ],
        scratch_shapes=[pltpu.VMEM((CONV_HALO_ROWS, D_MODEL), _F32)],
        compiler_params=pltpu.CompilerParams(
            dimension_semantics=("arbitrary",), vmem_limit_bytes=VMEM_LIMIT),
        name="proj",
    )(x, nm, wc, wqk, wvt, wg, cw, qn, kn, woa)


def _attn_kernel(q_ref, k_ref, vt_ref, kbar_ref, o_ref, bias_ref, *, n_blocks):
    i = pl.program_id(1)
    scale = HEAD_DIM ** -0.5
    q = q_ref[...]

    gate = _dot_nt(kbar_ref[...].astype(_BF), q)
    blk = lax.broadcasted_iota(jnp.int32, gate.shape, 0)
    gate = jnp.where(blk < i, gate, -jnp.inf)
    bias = jnp.full(gate.shape, NEG_BIG, _F32)
    for _ in range(MOBA_TOPK):
        top = jnp.max(gate, axis=0, keepdims=True)
        first = jnp.min(jnp.where(gate == top, blk, n_blocks), axis=0, keepdims=True)
        hit = blk == first
        bias = jnp.where(hit, 0.0, bias)
        gate = jnp.where(hit, -jnp.inf, gate)
    bias_ref[...] = bias

    s = _dot_nt(k_ref[i], q) * scale
    kpos = lax.broadcasted_iota(jnp.int32, s.shape, 0)
    qpos = lax.broadcasted_iota(jnp.int32, s.shape, 1)
    s = jnp.where(kpos <= qpos, s, NEG_BIG)
    m = jnp.max(s, axis=0, keepdims=True)
    p = jnp.exp(s - m)
    l = jnp.sum(p, axis=0, keepdims=True)
    acc = _dot(vt_ref[0, i], p.astype(_BF))

    def body(j, carry):
        m, l, acc = carry
        s = _dot_nt(k_ref[j], q) * scale + bias_ref[pl.ds(j, 1), :]
        m_new = jnp.maximum(m, jnp.max(s, axis=0, keepdims=True))
        alpha = jnp.exp(m - m_new)
        p = jnp.exp(s - m_new)
        l = alpha * l + jnp.sum(p, axis=0, keepdims=True)
        acc = alpha * acc + _dot(vt_ref[0, j], p.astype(_BF))
        return m_new, l, acc

    m, l, acc = lax.fori_loop(0, i, body, (m, l, acc))
    o_ref[...] = (acc * (1.0 / l)).T.astype(o_ref.dtype)


def _attn(q, k, vt, kbar):
    s = q.shape[0]
    nb = s // MOBA_BLOCK
    k3 = k.reshape(nb, MOBA_BLOCK, D_MODEL)
    return pl.pallas_call(
        functools.partial(_attn_kernel, n_blocks=nb),
        grid=(N_HEADS, nb),
        in_specs=[
            pl.BlockSpec((MOBA_BLOCK, HEAD_DIM), lambda h, i: (i, h)),
            pl.BlockSpec((nb, MOBA_BLOCK, HEAD_DIM), lambda h, i: (0, 0, h)),
            pl.BlockSpec((1, nb, HEAD_DIM, MOBA_BLOCK), lambda h, i: (h, 0, 0, 0)),
            pl.BlockSpec((nb, HEAD_DIM), lambda h, i: (0, h)),
        ],
        out_specs=pl.BlockSpec((MOBA_BLOCK, HEAD_DIM), lambda h, i: (i, h)),
        out_shape=jax.ShapeDtypeStruct((s, D_MODEL), _BF),
        scratch_shapes=[pltpu.VMEM((nb, MOBA_BLOCK), _F32)],
        compiler_params=pltpu.CompilerParams(
            dimension_semantics=("arbitrary", "arbitrary"), vmem_limit_bytes=VMEM_LIMIT),
        name="moba_attn",
    )(q, k3, vt, kbar)


def _post_kernel(o_ref, sa_ref, sgb_ref, x_ref, wob_ref, wo_ref, nf_ref, wup_ref, fcw_ref, fcb_ref,
                 wdn_ref, out_ref, halo_ref):
    i = pl.program_id(0)

    @pl.when(i == 0)
    def _():
        halo_ref[...] = jnp.zeros_like(halo_ref)

    branch_b = _dot(o_ref[...], wob_ref[...])
    merged = sa_ref[...] + sgb_ref[...] * branch_b
    h = x_ref[...] + _dot(merged.astype(_BF), wo_ref[...])
    hn = (_rms(h) * nf_ref[...]).astype(_BF)

    acc = h
    for c in range(D_FF // FFN_CHUNK):
        glo = c * FFN_CHUNK
        vlo = D_FF + glo
        ug = _dot(hn, wup_ref[:, glo:glo + FFN_CHUNK])
        uv = _dot(hn, wup_ref[:, vlo:vlo + FFN_CHUNK])
        g = (_causal_conv3(ug, halo_ref[:, glo:glo + FFN_CHUNK], fcw_ref[:, glo:glo + FFN_CHUNK])
             + fcb_ref[:, glo:glo + FFN_CHUNK])
        v = (_causal_conv3(uv, halo_ref[:, vlo:vlo + FFN_CHUNK], fcw_ref[:, vlo:vlo + FFN_CHUNK])
             + fcb_ref[:, vlo:vlo + FFN_CHUNK])
        halo_ref[:, glo:glo + FFN_CHUNK] = ug[POST_ROWS - CONV_HALO_ROWS:, :]
        halo_ref[:, vlo:vlo + FFN_CHUNK] = uv[POST_ROWS - CONV_HALO_ROWS:, :]
        act = (jax.nn.silu(g) * v).astype(_BF)
        acc = acc + _dot(act, wdn_ref[glo:glo + FFN_CHUNK, :])
    out_ref[...] = acc


def _post(o, sa, sgb, x, wob, wo, nf, wup, fcw, fcb, wdn):
    s = x.shape[0]
    const = lambda shape: pl.BlockSpec(shape, lambda i: (0,) * len(shape), pipeline_mode=pl.Buffered(1))
    rows = pl.BlockSpec((POST_ROWS, D_MODEL), lambda i: (i, 0))
    return pl.pallas_call(
        _post_kernel,
        grid=(s // POST_ROWS,),
        in_specs=[
            rows, rows, rows, rows,
            const((D_MODEL, D_MODEL)),
            const((D_MODEL, D_MODEL)),
            const((1, D_MODEL)),
            const((D_MODEL, 2 * D_FF)),
            const((3, 2 * D_FF)),
            const((1, 2 * D_FF)),
            const((D_FF, D_MODEL)),
        ],
        out_specs=rows,
        out_shape=jax.ShapeDtypeStruct((s, D_MODEL), _F32),
        scratch_shapes=[pltpu.VMEM((CONV_HALO_ROWS, 2 * D_FF), _F32)],
        compiler_params=pltpu.CompilerParams(
            dimension_semantics=("arbitrary",), vmem_limit_bytes=VMEM_LIMIT),
        name="post",
    )(o, sa, sgb, x, wob, wo, nf, wup, fcw, fcb, wdn)


def kernel(x, norm_mix, w_in, conv_a_w, q_norm, k_norm, w_out_a, w_out_b, w_o, norm_ffn, w_up,
           ffn_conv_w, ffn_conv_b, w_down):
    b, s, d = x.shape
    assert b == 1 and d == D_MODEL and s % POST_ROWS == 0 and s % MOBA_BLOCK == 0
    assert s // MOBA_BLOCK >= 8
    x2 = x.reshape(s, d)
    wb = w_in.astype(_BF)
    wc = wb[:, :3 * D_MODEL]
    wqk = wb[:, 3 * D_MODEL:5 * D_MODEL]
    wvt = wb[:, 5 * D_MODEL:6 * D_MODEL].T
    wg = wb[:, 6 * D_MODEL:]

    sa, sgb, q, k, vt, kbar = _proj(
        x2, norm_mix.reshape(1, d), wc, wqk, wvt, wg, conv_a_w, q_norm.reshape(1, HEAD_DIM),
        k_norm.reshape(1, HEAD_DIM), w_out_a.astype(_BF))
    o = _attn(q, k, vt, kbar.reshape(s // MOBA_BLOCK, d))
    out = _post(o, sa, sgb, x2, w_out_b.astype(_BF), w_o.astype(_BF), norm_ffn.reshape(1, d),
                w_up.astype(_BF), ffn_conv_w, ffn_conv_b.reshape(1, 2 * D_FF), w_down.astype(_BF))
    return out.reshape(b, s, d)
```

```python
import functools

import jax
import jax.numpy as jnp
from jax import lax
from jax.experimental import pallas as pl
from jax.experimental.pallas import tpu as pltpu

D_MODEL = 1024
N_HEADS = 8
HEAD_DIM = 128
MOBA_BLOCK = 256
MOBA_TOPK = 3
D_FF = 2816
EPS = 1e-6
CONV_HALO_ROWS = 8
NEG_BIG = -1e30

PROJ_ROWS = 256
POST_ROWS = 512
FFN_CHUNK = 256
VMEM_LIMIT = 56 * 1024 * 1024
ATTN_HEADS = 4
ONES_ROWS = 16
SCORE_LOOKAHEAD = 2
QK_SCALE = HEAD_DIM ** -0.5 * 1.4426950408889634

_BF = jnp.bfloat16
_F32 = jnp.float32


def _dot(a, b):
    return jnp.dot(a, b, preferred_element_type=_F32)


def _dot_nt(a, b):
    return lax.dot_general(a, b, (((1,), (1,)), ((), ())), preferred_element_type=_F32)


def _rms(x):
    return x * lax.rsqrt(jnp.mean(x * x, axis=-1, keepdims=True) + EPS)


def _causal_conv3(c, prev, w):
    row = lax.broadcasted_iota(jnp.int32, c.shape, 0)
    p1 = prev[CONV_HALO_ROWS - 1:CONV_HALO_ROWS, :]
    p2 = prev[CONV_HALO_ROWS - 2:CONV_HALO_ROWS - 1, :]
    c1 = jnp.where(row == 0, p1, pltpu.roll(c, 1, 0))
    c2 = jnp.where(row == 0, p2, jnp.where(row == 1, p1, pltpu.roll(c, 2, 0)))
    return w[0:1, :] * c2 + w[1:2, :] * c1 + w[2:3, :] * c


def _proj_kernel(x_ref, nm_ref, wc_ref, wqk_ref, wvt_ref, wg_ref, cw_ref, qn_ref, kn_ref, woa_ref,
                 sa_ref, sgb_ref, q_ref, k_ref, vt_ref, kbar_ref, halo_ref):
    i = pl.program_id(0)

    @pl.when(i == 0)
    def _():
        halo_ref[...] = jnp.zeros_like(halo_ref)

    xb = (_rms(x_ref[...]) * nm_ref[...]).astype(_BF)

    zc = _dot(xb, wc_ref[...])
    bg = zc[:, :D_MODEL]
    c = zc[:, D_MODEL:2 * D_MODEL] * zc[:, 2 * D_MODEL:]
    y = _causal_conv3(c, halo_ref[...], cw_ref[...])
    halo_ref[...] = c[PROJ_ROWS - CONV_HALO_ROWS:, :]
    branch_a = _dot((bg * y).astype(_BF), woa_ref[...])

    zg = _dot(xb, wg_ref[...])
    sa_ref[...] = jax.nn.sigmoid(zg[:, :D_MODEL]) * branch_a
    sgb_ref[...] = jax.nn.sigmoid(zg[:, D_MODEL:])

    zqk = _dot(xb, wqk_ref[...])
    for h in range(N_HEADS):
        lo = h * HEAD_DIM
        qh = _rms(zqk[:, lo:lo + HEAD_DIM]) * qn_ref[...] * QK_SCALE
        kh = _rms(zqk[:, D_MODEL + lo:D_MODEL + lo + HEAD_DIM]) * kn_ref[...]
        q_ref[:, lo:lo + HEAD_DIM] = qh.astype(_BF)
        k_ref[:, lo:lo + HEAD_DIM] = kh.astype(_BF)
        kbar_ref[0, :, lo:lo + HEAD_DIM] = jnp.mean(kh, axis=0, keepdims=True)

    vt = _dot_nt(wvt_ref[...], xb)
    for h in range(N_HEADS):
        vt_ref[h, 0] = vt[h * HEAD_DIM:(h + 1) * HEAD_DIM, :].astype(_BF)


def _proj(x, nm, wc, wqk, wvt, wg, cw, qn, kn, woa):
    s = x.shape[0]
    nb = s // PROJ_ROWS
    const = lambda shape: pl.BlockSpec(shape, lambda i: (0,) * len(shape), pipeline_mode=pl.Buffered(1))
    rows = lambda w: pl.BlockSpec((PROJ_ROWS, w), lambda i: (i, 0))
    return pl.pallas_call(
        _proj_kernel,
        grid=(nb,),
        in_specs=[
            rows(D_MODEL),
            const((1, D_MODEL)),
            const((D_MODEL, 3 * D_MODEL)),
            const((D_MODEL, 2 * D_MODEL)),
            const((D_MODEL, D_MODEL)),
            const((D_MODEL, 2 * D_MODEL)),
            const((3, D_MODEL)),
            const((1, HEAD_DIM)),
            const((1, HEAD_DIM)),
            const((D_MODEL, D_MODEL)),
        ],
        out_specs=[
            rows(D_MODEL),
            rows(D_MODEL),
            rows(D_MODEL),
            rows(D_MODEL),
            pl.BlockSpec((N_HEADS, 1, HEAD_DIM, MOBA_BLOCK), lambda i: (0, i, 0, 0)),
            pl.BlockSpec((1, 1, D_MODEL), lambda i: (i, 0, 0)),
        ],
        out_shape=[
            jax.ShapeDtypeStruct((s, D_MODEL), _F32),
            jax.ShapeDtypeStruct((s, D_MODEL), _F32),
            jax.ShapeDtypeStruct((s, D_MODEL), _BF),
            jax.ShapeDtypeStruct((s, D_MODEL), _BF),
            jax.ShapeDtypeStruct((N_HEADS, nb, HEAD_DIM, MOBA_BLOCK), _BF),
            jax.ShapeDtypeStruct((nb, 1, D_MODEL), _F32),
        ],
        scratch_shapes=[pltpu.VMEM((CONV_HALO_ROWS, D_MODEL), _F32)],
        compiler_params=pltpu.CompilerParams(
            dimension_semantics=("arbitrary",), vmem_limit_bytes=VMEM_LIMIT),
        name="proj",
    )(x, nm, wc, wqk, wvt, wg, cw, qn, kn, woa)


def _attn_kernel(q_ref, k_ref, vt_ref, kbar_ref, o_ref, bias_ref, m_ref, l_ref, acc_ref, *, n_blocks):
    i = pl.program_id(1)
    ones_rows = jnp.ones((ONES_ROWS, MOBA_BLOCK), _BF)

    def head_cols(hh):
        return slice(hh * HEAD_DIM, (hh + 1) * HEAD_DIM)

    def pv_and_sum(hh, j, p):
        out = _dot(jnp.concatenate([vt_ref[hh, j], ones_rows], axis=0), p)
        return out[:HEAD_DIM], out[HEAD_DIM:HEAD_DIM + 1]

    for hh in range(ATTN_HEADS):
        q = q_ref[:, head_cols(hh)]
        gate = _dot_nt(kbar_ref[:, head_cols(hh)].astype(_BF), q)
        blk = lax.broadcasted_iota(jnp.int32, gate.shape, 0)
        gate = jnp.where(blk < i, gate, -jnp.inf)
        bias = jnp.full(gate.shape, NEG_BIG, _F32)
        for _ in range(MOBA_TOPK):
            top = jnp.max(gate, axis=0, keepdims=True)
            first = jnp.min(jnp.where(gate == top, blk, n_blocks), axis=0, keepdims=True)
            hit = blk == first
            bias = jnp.where(hit, 0.0, bias)
            gate = jnp.where(hit, -jnp.inf, gate)
        bias_ref[hh] = bias

        s = _dot_nt(k_ref[i, :, head_cols(hh)], q)
        kpos = lax.broadcasted_iota(jnp.int32, s.shape, 0)
        qpos = lax.broadcasted_iota(jnp.int32, s.shape, 1)
        s = jnp.where(kpos <= qpos, s, NEG_BIG)
        m = jnp.max(s, axis=0, keepdims=True)
        pv, psum = pv_and_sum(hh, i, jnp.exp2(s - m).astype(_BF))
        m_ref[hh] = m
        l_ref[hh] = psum
        acc_ref[hh] = pv

    def body(t, carry):
        j0 = 2 * t
        j1 = j0 + 1

        def scores(hh):
            q = q_ref[:, head_cols(hh)]
            return (_dot_nt(k_ref[j0, :, head_cols(hh)], q) + bias_ref[hh, pl.ds(j0, 1), :],
                    _dot_nt(k_ref[j1, :, head_cols(hh)], q) + bias_ref[hh, pl.ds(j1, 1), :])

        def update(hh, s0, s1):
            m_old = m_ref[hh]
            m_new = jnp.maximum(m_old, jnp.max(jnp.maximum(s0, s1), axis=0, keepdims=True))
            alpha = jnp.exp2(m_old - m_new)
            pv0, psum0 = pv_and_sum(hh, j0, jnp.exp2(s0 - m_new).astype(_BF))
            pv1, psum1 = pv_and_sum(hh, j1, jnp.exp2(s1 - m_new).astype(_BF))
            m_ref[hh] = m_new
            l_ref[hh] = alpha * l_ref[hh] + (psum0 + psum1)
            acc_ref[hh] = alpha * acc_ref[hh] + (pv0 + pv1)

        pending = []
        for hh in range(ATTN_HEADS):
            pending.append((hh,) + scores(hh))
            if len(pending) > SCORE_LOOKAHEAD:
                update(*pending.pop(0))
        while pending:
            update(*pending.pop(0))
        return carry

    lax.fori_loop(0, lax.shift_right_logical(i + 1, 1), body, 0)
    for hh in range(ATTN_HEADS):
        o_ref[:, head_cols(hh)] = (acc_ref[hh] * (1.0 / l_ref[hh])).T.astype(o_ref.dtype)


def _attn(q, k, vt, kbar):
    s = q.shape[0]
    nb = s // MOBA_BLOCK
    k3 = k.reshape(nb, MOBA_BLOCK, D_MODEL)
    gw = ATTN_HEADS * HEAD_DIM
    return pl.pallas_call(
        functools.partial(_attn_kernel, n_blocks=nb),
        grid=(N_HEADS // ATTN_HEADS, nb),
        in_specs=[
            pl.BlockSpec((MOBA_BLOCK, gw), lambda g, i: (i, g)),
            pl.BlockSpec((nb, MOBA_BLOCK, gw), lambda g, i: (0, 0, g), pipeline_mode=pl.Buffered(1)),
            pl.BlockSpec((ATTN_HEADS, nb, HEAD_DIM, MOBA_BLOCK), lambda g, i: (g, 0, 0, 0),
                         pipeline_mode=pl.Buffered(1)),
            pl.BlockSpec((nb, gw), lambda g, i: (0, g)),
        ],
        out_specs=pl.BlockSpec((MOBA_BLOCK, gw), lambda g, i: (i, g)),
        out_shape=jax.ShapeDtypeStruct((s, D_MODEL), _BF),
        scratch_shapes=[
            pltpu.VMEM((ATTN_HEADS, nb, MOBA_BLOCK), _F32),
            pltpu.VMEM((ATTN_HEADS, 1, MOBA_BLOCK), _F32),
            pltpu.VMEM((ATTN_HEADS, 1, MOBA_BLOCK), _F32),
            pltpu.VMEM((ATTN_HEADS, HEAD_DIM, MOBA_BLOCK), _F32),
        ],
        compiler_params=pltpu.CompilerParams(
            dimension_semantics=("arbitrary", "arbitrary"), vmem_limit_bytes=VMEM_LIMIT),
        name="moba_attn",
    )(q, k3, vt, kbar)


def _post_kernel(o_ref, sa_ref, sgb_ref, x_ref, wob_ref, wo_ref, nf_ref, wup_ref, fcw_ref, fcb_ref,
                 wdn_ref, out_ref, halo_ref):
    i = pl.program_id(0)

    @pl.when(i == 0)
    def _():
        halo_ref[...] = jnp.zeros_like(halo_ref)

    branch_b = _dot(o_ref[...], wob_ref[...])
    merged = sa_ref[...] + sgb_ref[...] * branch_b
    h = x_ref[...] + _dot(merged.astype(_BF), wo_ref[...])
    hn = (_rms(h) * nf_ref[...]).astype(_BF)

    acc = h
    for c in range(D_FF // FFN_CHUNK):
        glo = c * FFN_CHUNK
        vlo = D_FF + glo
        ug = _dot(hn, wup_ref[:, glo:glo + FFN_CHUNK])
        uv = _dot(hn, wup_ref[:, vlo:vlo + FFN_CHUNK])
        g = (_causal_conv3(ug, halo_ref[:, glo:glo + FFN_CHUNK], fcw_ref[:, glo:glo + FFN_CHUNK])
             + fcb_ref[:, glo:glo + FFN_CHUNK])
        v = (_causal_conv3(uv, halo_ref[:, vlo:vlo + FFN_CHUNK], fcw_ref[:, vlo:vlo + FFN_CHUNK])
             + fcb_ref[:, vlo:vlo + FFN_CHUNK])
        halo_ref[:, glo:glo + FFN_CHUNK] = ug[POST_ROWS - CONV_HALO_ROWS:, :]
        halo_ref[:, vlo:vlo + FFN_CHUNK] = uv[POST_ROWS - CONV_HALO_ROWS:, :]
        act = (jax.nn.silu(g) * v).astype(_BF)
        acc = acc + _dot(act, wdn_ref[glo:glo + FFN_CHUNK, :])
    out_ref[...] = acc


def _post(o, sa, sgb, x, wob, wo, nf, wup, fcw, fcb, wdn):
    s = x.shape[0]
    const = lambda shape: pl.BlockSpec(shape, lambda i: (0,) * len(shape), pipeline_mode=pl.Buffered(1))
    rows = pl.BlockSpec((POST_ROWS, D_MODEL), lambda i: (i, 0))
    return pl.pallas_call(
        _post_kernel,
        grid=(s // POST_ROWS,),
        in_specs=[
            rows, rows, rows, rows,
            const((D_MODEL, D_MODEL)),
            const((D_MODEL, D_MODEL)),
            const((1, D_MODEL)),
            const((D_MODEL, 2 * D_FF)),
            const((3, 2 * D_FF)),
            const((1, 2 * D_FF)),
            const((D_FF, D_MODEL)),
        ],
        out_specs=rows,
        out_shape=jax.ShapeDtypeStruct((s, D_MODEL), _F32),
        scratch_shapes=[pltpu.VMEM((CONV_HALO_ROWS, 2 * D_FF), _F32)],
        compiler_params=pltpu.CompilerParams(
            dimension_semantics=("arbitrary",), vmem_limit_bytes=VMEM_LIMIT),
        name="post",
    )(o, sa, sgb, x, wob, wo, nf, wup, fcw, fcb, wdn)


def kernel(x, norm_mix, w_in, conv_a_w, q_norm, k_norm, w_out_a, w_out_b, w_o, norm_ffn, w_up,
           ffn_conv_w, ffn_conv_b, w_down):
    b, s, d = x.shape
    assert b == 1 and d == D_MODEL and s % POST_ROWS == 0 and s % MOBA_BLOCK == 0
    assert s // MOBA_BLOCK >= 8
    x2 = x.reshape(s, d)
    wb = w_in.astype(_BF)
    wc = wb[:, :3 * D_MODEL]
    wqk = wb[:, 3 * D_MODEL:5 * D_MODEL]
    wvt = wb[:, 5 * D_MODEL:6 * D_MODEL].T
    wg = wb[:, 6 * D_MODEL:]

    sa, sgb, q, k, vt, kbar = _proj(
        x2, norm_mix.reshape(1, d), wc, wqk, wvt, wg, conv_a_w, q_norm.reshape(1, HEAD_DIM),
        k_norm.reshape(1, HEAD_DIM), w_out_a.astype(_BF))
    o = _attn(q, k, vt, kbar.reshape(s // MOBA_BLOCK, d))
    out = _post(o, sa, sgb, x2, w_out_b.astype(_BF), w_o.astype(_BF), norm_ffn.reshape(1, d),
                w_up.astype(_BF), ffn_conv_w, ffn_conv_b.reshape(1, 2 * D_FF), w_down.astype(_BF))
    return out.reshape(b, s, d)
```

```python
import functools

import jax
import jax.numpy as jnp
from jax import lax
from jax.experimental import pallas as pl
from jax.experimental.pallas import tpu as pltpu

D_MODEL = 1024
N_HEADS = 8
HEAD_DIM = 128
MOBA_BLOCK = 256
MOBA_TOPK = 3
D_FF = 2816
EPS = 1e-6
CONV_HALO_ROWS = 8
NEG_BIG = -1e30

PROJ_ROWS = 256
POST_ROWS = 512
FFN_CHUNK = 256
VMEM_LIMIT = 56 * 1024 * 1024
ATTN_HEADS = 4
ONES_ROWS = 16
UNIT_SHIFT = 1
UNIT_BLOCKS = 1 << UNIT_SHIFT
UNIT_KEYS = UNIT_BLOCKS * MOBA_BLOCK
QK_DEPTH = 2 * HEAD_DIM
QK_SCALE = HEAD_DIM ** -0.5 * 1.4426950408889634

_BF = jnp.bfloat16
_F32 = jnp.float32


def _dot(a, b):
    return jnp.dot(a, b, preferred_element_type=_F32)


def _dot_nt(a, b):
    return lax.dot_general(a, b, (((1,), (1,)), ((), ())), preferred_element_type=_F32)


def _rms(x, axis=-1):
    return x * lax.rsqrt(jnp.mean(x * x, axis=axis, keepdims=True) + EPS)


def _causal_conv3(c, prev, w):
    row = lax.broadcasted_iota(jnp.int32, c.shape, 0)
    p1 = prev[CONV_HALO_ROWS - 1:CONV_HALO_ROWS, :]
    p2 = prev[CONV_HALO_ROWS - 2:CONV_HALO_ROWS - 1, :]
    c1 = jnp.where(row == 0, p1, pltpu.roll(c, 1, 0))
    c2 = jnp.where(row == 0, p2, jnp.where(row == 1, p1, pltpu.roll(c, 2, 0)))
    return w[0:1, :] * c2 + w[1:2, :] * c1 + w[2:3, :] * c


def _proj_kernel(x_ref, nm_ref, wc_ref, wqt_ref, wk_ref, wvt_ref, wg_ref, cw_ref, qn_ref, kn_ref, woa_ref,
                 sa_ref, sgb_ref, qt_ref, k_ref, vt_ref, kbar_ref, halo_ref):
    i = pl.program_id(0)

    @pl.when(i == 0)
    def _():
        halo_ref[...] = jnp.zeros_like(halo_ref)

    xb = (_rms(x_ref[...]) * nm_ref[...]).astype(_BF)

    zc = _dot(xb, wc_ref[...])
    bg = zc[:, :D_MODEL]
    c = zc[:, D_MODEL:2 * D_MODEL] * zc[:, 2 * D_MODEL:]
    y = _causal_conv3(c, halo_ref[...], cw_ref[...])
    halo_ref[...] = c[PROJ_ROWS - CONV_HALO_ROWS:, :]
    branch_a = _dot((bg * y).astype(_BF), woa_ref[...])

    zg = _dot(xb, wg_ref[...])
    sa_ref[...] = jax.nn.sigmoid(zg[:, :D_MODEL]) * branch_a
    sgb_ref[...] = jax.nn.sigmoid(zg[:, D_MODEL:])

    zk = _dot(xb, wk_ref[...])
    for h in range(N_HEADS):
        lo = h * HEAD_DIM
        kh = _rms(zk[:, lo:lo + HEAD_DIM]) * kn_ref[...]
        k_ref[:, lo:lo + HEAD_DIM] = kh.astype(_BF)
        kbar_ref[0, :, lo:lo + HEAD_DIM] = jnp.mean(kh, axis=0, keepdims=True)

    zqt = _dot_nt(wqt_ref[...], xb)
    for h in range(N_HEADS):
        lo = h * HEAD_DIM
        qh = _rms(zqt[lo:lo + HEAD_DIM, :], axis=0) * qn_ref[...] * QK_SCALE
        qt_ref[lo:lo + HEAD_DIM, :] = qh.astype(_BF)

    vt = _dot_nt(wvt_ref[...], xb)
    for h in range(N_HEADS):
        vt_ref[h, 0] = vt[h * HEAD_DIM:(h + 1) * HEAD_DIM, :].astype(_BF)


def _proj(x, nm, wc, wqt, wk, wvt, wg, cw, qn, kn, woa):
    s = x.shape[0]
    nb = s // PROJ_ROWS
    const = lambda shape: pl.BlockSpec(shape, lambda i: (0,) * len(shape), pipeline_mode=pl.Buffered(1))
    rows = lambda w: pl.BlockSpec((PROJ_ROWS, w), lambda i: (i, 0))
    return pl.pallas_call(
        _proj_kernel,
        grid=(nb,),
        in_specs=[
            rows(D_MODEL),
            const((1, D_MODEL)),
            const((D_MODEL, 3 * D_MODEL)),
            const((D_MODEL, D_MODEL)),
            const((D_MODEL, D_MODEL)),
            const((D_MODEL, D_MODEL)),
            const((D_MODEL, 2 * D_MODEL)),
            const((3, D_MODEL)),
            const((HEAD_DIM, 1)),
            const((1, HEAD_DIM)),
            const((D_MODEL, D_MODEL)),
        ],
        out_specs=[
            rows(D_MODEL),
            rows(D_MODEL),
            pl.BlockSpec((D_MODEL, PROJ_ROWS), lambda i: (0, i)),
            rows(D_MODEL),
            pl.BlockSpec((N_HEADS, 1, HEAD_DIM, MOBA_BLOCK), lambda i: (0, i, 0, 0)),
            pl.BlockSpec((1, 1, D_MODEL), lambda i: (i, 0, 0)),
        ],
        out_shape=[
            jax.ShapeDtypeStruct((s, D_MODEL), _F32),
            jax.ShapeDtypeStruct((s, D_MODEL), _F32),
            jax.ShapeDtypeStruct((D_MODEL, s), _BF),
            jax.ShapeDtypeStruct((s, D_MODEL), _BF),
            jax.ShapeDtypeStruct((N_HEADS, nb, HEAD_DIM, MOBA_BLOCK), _BF),
            jax.ShapeDtypeStruct((nb, 1, D_MODEL), _F32),
        ],
        scratch_shapes=[pltpu.VMEM((CONV_HALO_ROWS, D_MODEL), _F32)],
        compiler_params=pltpu.CompilerParams(
            dimension_semantics=("arbitrary",), vmem_limit_bytes=VMEM_LIMIT),
        name="proj",
    )(x, nm, wc, wqt, wk, wvt, wg, cw, qn, kn, woa)


def _attn_kernel(qt_ref, k_ref, vt_ref, kbar_ref, o_ref, qaug_ref, sa_ref, sb_ref, mxa_ref, mxb_ref,
                 m_ref, l_ref, acc_ref, *, n_blocks):
    i = pl.program_id(1)
    ones_rows = jnp.ones((ONES_ROWS, MOBA_BLOCK), _BF)
    pad_rows = jnp.zeros((QK_DEPTH - HEAD_DIM - n_blocks, MOBA_BLOCK), _BF)
    lane = lax.broadcasted_iota(jnp.int32, (ONES_ROWS, HEAD_DIM), 1)

    def head_rows(hh):
        return slice(hh * HEAD_DIM, (hh + 1) * HEAD_DIM)

    def pv_and_sum(hh, j, p):
        out = _dot(jnp.concatenate([vt_ref[hh, j], ones_rows], axis=0), p)
        return out[:HEAD_DIM], out[HEAD_DIM:HEAD_DIM + 1]

    for hh in range(ATTN_HEADS):
        qt = qt_ref[head_rows(hh), :]
        sa_ref[hh, :n_blocks, :] = _dot(kbar_ref[:, head_rows(hh)].astype(_BF), qt)
        sb_ref[hh, :MOBA_BLOCK, :] = _dot(k_ref[i, :, head_rows(hh)], qt)

    for hh in range(ATTN_HEADS):
        gate = sa_ref[hh, :n_blocks, :]
        blk = lax.broadcasted_iota(jnp.int32, gate.shape, 0)
        gate = jnp.where(blk < i, gate, -jnp.inf)
        bias = jnp.full(gate.shape, NEG_BIG, _F32)
        for _ in range(MOBA_TOPK):
            top = jnp.max(gate, axis=0, keepdims=True)
            first = jnp.min(jnp.where(gate == top, blk, n_blocks), axis=0, keepdims=True)
            hit = blk == first
            bias = jnp.where(hit, 0.0, bias)
            gate = jnp.where(hit, -jnp.inf, gate)
        qaug_ref[hh] = jnp.concatenate([qt_ref[head_rows(hh), :], bias.astype(_BF), pad_rows], axis=0)

    def unit_blocks(u):
        j0 = jnp.minimum(UNIT_BLOCKS * u, n_blocks - UNIT_BLOCKS)
        return [j0 + b for b in range(UNIT_BLOCKS)]

    def block_onehot(j):
        row = (lane == j).astype(_BF)
        return jnp.concatenate([row] * (MOBA_BLOCK // ONES_ROWS), axis=0)

    def scores(u, hh, dst_ref, mx_ref):
        ka = jnp.concatenate(
            [jnp.concatenate([k_ref[j, :, head_rows(hh)], block_onehot(j)], axis=1) for j in unit_blocks(u)],
            axis=0)
        s = _dot(ka, qaug_ref[hh])
        dst_ref[hh] = s
        mx_ref[hh] = jnp.max(s, axis=0, keepdims=True)

    def update(u, hh, src_ref, mx_ref):
        m_old = m_ref[hh]
        m_new = jnp.maximum(m_old, mx_ref[hh])
        alpha = jnp.exp2(m_old - m_new)
        parts = [pv_and_sum(hh, j, jnp.exp2(src_ref[hh, b * MOBA_BLOCK:(b + 1) * MOBA_BLOCK, :] - m_new).astype(_BF))
                 for b, j in enumerate(unit_blocks(u))]
        m_ref[hh] = m_new
        l_ref[hh] = alpha * l_ref[hh] + functools.reduce(jnp.add, [p[1] for p in parts])
        acc_ref[hh] = alpha * acc_ref[hh] + functools.reduce(jnp.add, [p[0] for p in parts])

    for hh in range(ATTN_HEADS):
        s = sb_ref[hh, :MOBA_BLOCK, :]
        kpos = lax.broadcasted_iota(jnp.int32, s.shape, 0)
        qpos = lax.broadcasted_iota(jnp.int32, s.shape, 1)
        s = jnp.where(kpos <= qpos, s, NEG_BIG)
        m = jnp.max(s, axis=0, keepdims=True)
        pv, psum = pv_and_sum(hh, i, jnp.exp2(s - m).astype(_BF))
        m_ref[hh] = m
        l_ref[hh] = psum
        acc_ref[hh] = pv
        scores(0, hh, sa_ref, mxa_ref)

    def body(v, carry):
        for hh in range(ATTN_HEADS):
            scores(2 * v + 1, hh, sb_ref, mxb_ref)
            update(2 * v, hh, sa_ref, mxa_ref)
        for hh in range(ATTN_HEADS):
            scores(2 * v + 2, hh, sa_ref, mxa_ref)
            update(2 * v + 1, hh, sb_ref, mxb_ref)
        return carry

    n_units = lax.shift_right_logical(i + (UNIT_BLOCKS - 1), UNIT_SHIFT)
    lax.fori_loop(0, lax.shift_right_logical(n_units + 1, 1), body, 0)
    for hh in range(ATTN_HEADS):
        o_ref[:, head_rows(hh)] = (acc_ref[hh] * (1.0 / l_ref[hh])).T.astype(o_ref.dtype)


def _attn(qt, k, vt, kbar):
    s = k.shape[0]
    nb = s // MOBA_BLOCK
    assert nb <= QK_DEPTH - HEAD_DIM and nb % (2 * UNIT_BLOCKS) == 0 and nb <= UNIT_KEYS
    k3 = k.reshape(nb, MOBA_BLOCK, D_MODEL)
    gw = ATTN_HEADS * HEAD_DIM
    scores_buf = pltpu.VMEM((ATTN_HEADS, UNIT_KEYS, MOBA_BLOCK), _F32)
    row_buf = pltpu.VMEM((ATTN_HEADS, 1, MOBA_BLOCK), _F32)
    return pl.pallas_call(
        functools.partial(_attn_kernel, n_blocks=nb),
        grid=(N_HEADS // ATTN_HEADS, nb),
        in_specs=[
            pl.BlockSpec((gw, MOBA_BLOCK), lambda g, i: (g, i)),
            pl.BlockSpec((nb, MOBA_BLOCK, gw), lambda g, i: (0, 0, g), pipeline_mode=pl.Buffered(1)),
            pl.BlockSpec((ATTN_HEADS, nb, HEAD_DIM, MOBA_BLOCK), lambda g, i: (g, 0, 0, 0),
                         pipeline_mode=pl.Buffered(1)),
            pl.BlockSpec((nb, gw), lambda g, i: (0, g)),
        ],
        out_specs=pl.BlockSpec((MOBA_BLOCK, gw), lambda g, i: (i, g)),
        out_shape=jax.ShapeDtypeStruct((s, D_MODEL), _BF),
        scratch_shapes=[
            pltpu.VMEM((ATTN_HEADS, QK_DEPTH, MOBA_BLOCK), _BF),
            scores_buf,
            scores_buf,
            row_buf,
            row_buf,
            row_buf,
            row_buf,
            pltpu.VMEM((ATTN_HEADS, HEAD_DIM, MOBA_BLOCK), _F32),
        ],
        compiler_params=pltpu.CompilerParams(
            dimension_semantics=("arbitrary", "arbitrary"), vmem_limit_bytes=VMEM_LIMIT),
        name="moba_attn",
    )(qt, k3, vt, kbar)


def _post_kernel(o_ref, sa_ref, sgb_ref, x_ref, wob_ref, wo_ref, nf_ref, wup_ref, fcw_ref, fcb_ref,
                 wdn_ref, out_ref, halo_ref):
    i = pl.program_id(0)

    @pl.when(i == 0)
    def _():
        halo_ref[...] = jnp.zeros_like(halo_ref)

    branch_b = _dot(o_ref[...], wob_ref[...])
    merged = sa_ref[...] + sgb_ref[...] * branch_b
    h = x_ref[...] + _dot(merged.astype(_BF), wo_ref[...])
    hn = (_rms(h) * nf_ref[...]).astype(_BF)

    acc = h
    for c in range(D_FF // FFN_CHUNK):
        glo = c * FFN_CHUNK
        vlo = D_FF + glo
        ug = _dot(hn, wup_ref[:, glo:glo + FFN_CHUNK])
        uv = _dot(hn, wup_ref[:, vlo:vlo + FFN_CHUNK])
        g = (_causal_conv3(ug, halo_ref[:, glo:glo + FFN_CHUNK], fcw_ref[:, glo:glo + FFN_CHUNK])
             + fcb_ref[:, glo:glo + FFN_CHUNK])
        v = (_causal_conv3(uv, halo_ref[:, vlo:vlo + FFN_CHUNK], fcw_ref[:, vlo:vlo + FFN_CHUNK])
             + fcb_ref[:, vlo:vlo + FFN_CHUNK])
        halo_ref[:, glo:glo + FFN_CHUNK] = ug[POST_ROWS - CONV_HALO_ROWS:, :]
        halo_ref[:, vlo:vlo + FFN_CHUNK] = uv[POST_ROWS - CONV_HALO_ROWS:, :]
        act = (jax.nn.silu(g) * v).astype(_BF)
        acc = acc + _dot(act, wdn_ref[glo:glo + FFN_CHUNK, :])
    out_ref[...] = acc


def _post(o, sa, sgb, x, wob, wo, nf, wup, fcw, fcb, wdn):
    s = x.shape[0]
    const = lambda shape: pl.BlockSpec(shape, lambda i: (0,) * len(shape), pipeline_mode=pl.Buffered(1))
    rows = pl.BlockSpec((POST_ROWS, D_MODEL), lambda i: (i, 0))
    return pl.pallas_call(
        _post_kernel,
        grid=(s // POST_ROWS,),
        in_specs=[
            rows, rows, rows, rows,
            const((D_MODEL, D_MODEL)),
            const((D_MODEL, D_MODEL)),
            const((1, D_MODEL)),
            const((D_MODEL, 2 * D_FF)),
            const((3, 2 * D_FF)),
            const((1, 2 * D_FF)),
            const((D_FF, D_MODEL)),
        ],
        out_specs=rows,
        out_shape=jax.ShapeDtypeStruct((s, D_MODEL), _F32),
        scratch_shapes=[pltpu.VMEM((CONV_HALO_ROWS, 2 * D_FF), _F32)],
        compiler_params=pltpu.CompilerParams(
            dimension_semantics=("arbitrary",), vmem_limit_bytes=VMEM_LIMIT),
        name="post",
    )(o, sa, sgb, x, wob, wo, nf, wup, fcw, fcb, wdn)


def kernel(x, norm_mix, w_in, conv_a_w, q_norm, k_norm, w_out_a, w_out_b, w_o, norm_ffn, w_up,
           ffn_conv_w, ffn_conv_b, w_down):
    b, s, d = x.shape
    assert b == 1 and d == D_MODEL and s % POST_ROWS == 0 and s % MOBA_BLOCK == 0
    x2 = x.reshape(s, d)
    wb = w_in.astype(_BF)
    wc = wb[:, :3 * D_MODEL]
    wqt = wb[:, 3 * D_MODEL:4 * D_MODEL].T
    wk = wb[:, 4 * D_MODEL:5 * D_MODEL]
    wvt = wb[:, 5 * D_MODEL:6 * D_MODEL].T
    wg = wb[:, 6 * D_MODEL:]

    sa, sgb, qt, k, vt, kbar = _proj(
        x2, norm_mix.reshape(1, d), wc, wqt, wk, wvt, wg, conv_a_w, q_norm.reshape(HEAD_DIM, 1),
        k_norm.reshape(1, HEAD_DIM), w_out_a.astype(_BF))
    o = _attn(qt, k, vt, kbar.reshape(s // MOBA_BLOCK, d))
    out = _post(o, sa, sgb, x2, w_out_b.astype(_BF), w_o.astype(_BF), norm_ffn.reshape(1, d),
                w_up.astype(_BF), ffn_conv_w, ffn_conv_b.reshape(1, 2 * D_FF), w_down.astype(_BF))
    return out.reshape(b, s, d)
```

```python
import functools

import jax
import jax.numpy as jnp
from jax import lax
from jax.experimental import pallas as pl
from jax.experimental.pallas import tpu as pltpu

D_MODEL = 1024
N_HEADS = 8
HEAD_DIM = 128
MOBA_BLOCK = 256
MOBA_TOPK = 3
D_FF = 2816
EPS = 1e-6
CONV_HALO_ROWS = 8
NEG_BIG = -1e30

PROJ_ROWS = 256
POST_ROWS = 512
FFN_CHUNK = 256
FFN_LOOKAHEAD = 2
VMEM_LIMIT = 56 * 1024 * 1024
ATTN_HEADS = 4
ONES_ROWS = 16
UNIT_SHIFT = 1
UNIT_BLOCKS = 1 << UNIT_SHIFT
UNIT_KEYS = UNIT_BLOCKS * MOBA_BLOCK
QK_DEPTH = 2 * HEAD_DIM
QK_SCALE = HEAD_DIM ** -0.5 * 1.4426950408889634

_BF = jnp.bfloat16
_F32 = jnp.float32


def _dot(a, b):
    return jnp.dot(a, b, preferred_element_type=_F32)


def _dot_nt(a, b):
    return lax.dot_general(a, b, (((1,), (1,)), ((), ())), preferred_element_type=_F32)


def _rms(x, axis=-1):
    return x * lax.rsqrt(jnp.mean(x * x, axis=axis, keepdims=True) + EPS)


def _causal_conv3(c, prev, w):
    row = lax.broadcasted_iota(jnp.int32, c.shape, 0)
    p1 = prev[CONV_HALO_ROWS - 1:CONV_HALO_ROWS, :]
    p2 = prev[CONV_HALO_ROWS - 2:CONV_HALO_ROWS - 1, :]
    c1 = jnp.where(row == 0, p1, pltpu.roll(c, 1, 0))
    c2 = jnp.where(row == 0, p2, jnp.where(row == 1, p1, pltpu.roll(c, 2, 0)))
    return w[0:1, :] * c2 + w[1:2, :] * c1 + w[2:3, :] * c


def _proj_kernel(x_ref, nm_ref, wc_ref, wqt_ref, wk_ref, wvt_ref, wg_ref, cw_ref, qn_ref, kn_ref, woa_ref,
                 sa_ref, sgb_ref, qt_ref, k_ref, vt_ref, bias_ref, halo_ref, kbar_ref):
    i = pl.program_id(0)

    @pl.when(i == 0)
    def _():
        halo_ref[...] = jnp.zeros_like(halo_ref)
        kbar_ref[...] = jnp.zeros_like(kbar_ref)

    xb = (_rms(x_ref[...]) * nm_ref[...]).astype(_BF)
    n_blocks = kbar_ref.shape[0]

    zk = _dot(xb, wk_ref[...])
    zqt = _dot_nt(wqt_ref[...], xb)

    gates = []
    for h in range(N_HEADS):
        lo = h * HEAD_DIM
        qh = (_rms(zqt[lo:lo + HEAD_DIM, :], axis=0) * qn_ref[...] * QK_SCALE).astype(_BF)
        qt_ref[lo:lo + HEAD_DIM, :] = qh
        gates.append(_dot(kbar_ref[:, lo:lo + HEAD_DIM].astype(_BF), qh))

    zc = _dot(xb, wc_ref[...])

    for h in range(N_HEADS):
        gate = gates[h]
        blk = lax.broadcasted_iota(jnp.int32, gate.shape, 0)
        gate = jnp.where(blk < i, gate, -jnp.inf)
        bias = jnp.full(gate.shape, NEG_BIG, _F32)
        for _ in range(MOBA_TOPK):
            top = jnp.max(gate, axis=0, keepdims=True)
            first = jnp.min(jnp.where(gate == top, blk, n_blocks), axis=0, keepdims=True)
            hit = blk == first
            bias = jnp.where(hit, 0.0, bias)
            gate = jnp.where(hit, -jnp.inf, gate)
        bias_ref[h] = bias.astype(_BF)

    kbar_rows = []
    for h in range(N_HEADS):
        lo = h * HEAD_DIM
        kh = _rms(zk[:, lo:lo + HEAD_DIM]) * kn_ref[...]
        k_ref[:, lo:lo + HEAD_DIM] = kh.astype(_BF)
        kbar_rows.append(jnp.mean(kh, axis=0, keepdims=True))
    kbar_all = kbar_ref[...]
    blk_row = lax.broadcasted_iota(jnp.int32, kbar_all.shape, 0)
    kbar_ref[...] = jnp.where(blk_row == i, jnp.concatenate(kbar_rows, axis=1), kbar_all)

    zg = _dot(xb, wg_ref[...])

    bg = zc[:, :D_MODEL]
    c = zc[:, D_MODEL:2 * D_MODEL] * zc[:, 2 * D_MODEL:]
    y = _causal_conv3(c, halo_ref[...], cw_ref[...])
    halo_ref[...] = c[PROJ_ROWS - CONV_HALO_ROWS:, :]
    branch_a = _dot((bg * y).astype(_BF), woa_ref[...])

    vt = _dot_nt(wvt_ref[...], xb)

    sa_ref[...] = jax.nn.sigmoid(zg[:, :D_MODEL]) * branch_a
    sgb_ref[...] = jax.nn.sigmoid(zg[:, D_MODEL:])
    for h in range(N_HEADS):
        vt_ref[h, 0] = vt[h * HEAD_DIM:(h + 1) * HEAD_DIM, :].astype(_BF)


def _proj(x, nm, wc, wqt, wk, wvt, wg, cw, qn, kn, woa):
    s = x.shape[0]
    assert PROJ_ROWS == MOBA_BLOCK
    nb = s // PROJ_ROWS
    const = lambda shape: pl.BlockSpec(shape, lambda i: (0,) * len(shape), pipeline_mode=pl.Buffered(1))
    rows = lambda w: pl.BlockSpec((PROJ_ROWS, w), lambda i: (i, 0))
    return pl.pallas_call(
        _proj_kernel,
        grid=(nb,),
        in_specs=[
            rows(D_MODEL),
            const((1, D_MODEL)),
            const((D_MODEL, 3 * D_MODEL)),
            const((D_MODEL, D_MODEL)),
            const((D_MODEL, D_MODEL)),
            const((D_MODEL, D_MODEL)),
            const((D_MODEL, 2 * D_MODEL)),
            const((3, D_MODEL)),
            const((HEAD_DIM, 1)),
            const((1, HEAD_DIM)),
            const((D_MODEL, D_MODEL)),
        ],
        out_specs=[
            rows(D_MODEL),
            rows(D_MODEL),
            pl.BlockSpec((D_MODEL, PROJ_ROWS), lambda i: (0, i)),
            rows(D_MODEL),
            pl.BlockSpec((N_HEADS, 1, HEAD_DIM, MOBA_BLOCK), lambda i: (0, i, 0, 0)),
            pl.BlockSpec((N_HEADS, nb, PROJ_ROWS), lambda i: (0, 0, i)),
        ],
        out_shape=[
            jax.ShapeDtypeStruct((s, D_MODEL), _F32),
            jax.ShapeDtypeStruct((s, D_MODEL), _F32),
            jax.ShapeDtypeStruct((D_MODEL, s), _BF),
            jax.ShapeDtypeStruct((s, D_MODEL), _BF),
            jax.ShapeDtypeStruct((N_HEADS, nb, HEAD_DIM, MOBA_BLOCK), _BF),
            jax.ShapeDtypeStruct((N_HEADS, nb, s), _BF),
        ],
        scratch_shapes=[
            pltpu.VMEM((CONV_HALO_ROWS, D_MODEL), _F32),
            pltpu.VMEM((nb, D_MODEL), _F32),
        ],
        compiler_params=pltpu.CompilerParams(
            dimension_semantics=("arbitrary",), vmem_limit_bytes=VMEM_LIMIT),
        name="proj",
    )(x, nm, wc, wqt, wk, wvt, wg, cw, qn, kn, woa)


def _attn_kernel(qt_ref, k_ref, vt_ref, bias_ref, o_ref, qaug_ref, sa_ref, sb_ref, mxa_ref, mxb_ref,
                 m_ref, l_ref, acc_ref, *, n_blocks):
    i = pl.program_id(1)
    ones_rows = jnp.ones((ONES_ROWS, MOBA_BLOCK), _BF)
    pad_rows = jnp.zeros((QK_DEPTH - HEAD_DIM - n_blocks, MOBA_BLOCK), _BF)
    lane = lax.broadcasted_iota(jnp.int32, (ONES_ROWS, HEAD_DIM), 1)

    def head_rows(hh):
        return slice(hh * HEAD_DIM, (hh + 1) * HEAD_DIM)

    def pv_and_sum(hh, j, p):
        out = _dot(jnp.concatenate([vt_ref[hh, j], ones_rows], axis=0), p)
        return out[:HEAD_DIM], out[HEAD_DIM:HEAD_DIM + 1]

    for hh in range(ATTN_HEADS):
        qt = qt_ref[head_rows(hh), :]
        sb_ref[hh, :MOBA_BLOCK, :] = _dot(k_ref[i, :, head_rows(hh)], qt)
        qaug_ref[hh] = jnp.concatenate([qt, bias_ref[hh], pad_rows], axis=0)

    def unit_blocks(u):
        j0 = jnp.minimum(UNIT_BLOCKS * u, n_blocks - UNIT_BLOCKS)
        return [j0 + b for b in range(UNIT_BLOCKS)]

    def block_onehot(j):
        row = (lane == j).astype(_BF)
        return jnp.concatenate([row] * (MOBA_BLOCK // ONES_ROWS), axis=0)

    def scores(u, hh, dst_ref, mx_ref):
        ka = jnp.concatenate(
            [jnp.concatenate([k_ref[j, :, head_rows(hh)], block_onehot(j)], axis=1) for j in unit_blocks(u)],
            axis=0)
        s = _dot(ka, qaug_ref[hh])
        dst_ref[hh] = s
        mx_ref[hh] = jnp.max(s, axis=0, keepdims=True)

    def update(u, hh, src_ref, mx_ref):
        m_old = m_ref[hh]
        m_new = jnp.maximum(m_old, mx_ref[hh])
        alpha = jnp.exp2(m_old - m_new)
        parts = [pv_and_sum(hh, j, jnp.exp2(src_ref[hh, b * MOBA_BLOCK:(b + 1) * MOBA_BLOCK, :] - m_new).astype(_BF))
                 for b, j in enumerate(unit_blocks(u))]
        m_ref[hh] = m_new
        l_ref[hh] = alpha * l_ref[hh] + functools.reduce(jnp.add, [p[1] for p in parts])
        acc_ref[hh] = alpha * acc_ref[hh] + functools.reduce(jnp.add, [p[0] for p in parts])

    for hh in range(ATTN_HEADS):
        scores(0, hh, sa_ref, mxa_ref)

    for hh in range(ATTN_HEADS):
        s = sb_ref[hh, :MOBA_BLOCK, :]
        kpos = lax.broadcasted_iota(jnp.int32, s.shape, 0)
        qpos = lax.broadcasted_iota(jnp.int32, s.shape, 1)
        s = jnp.where(kpos <= qpos, s, NEG_BIG)
        m = jnp.max(s, axis=0, keepdims=True)
        pv, psum = pv_and_sum(hh, i, jnp.exp2(s - m).astype(_BF))
        m_ref[hh] = m
        l_ref[hh] = psum
        acc_ref[hh] = pv

    def body(v, carry):
        for hh in range(ATTN_HEADS):
            scores(2 * v + 1, hh, sb_ref, mxb_ref)
            update(2 * v, hh, sa_ref, mxa_ref)
        for hh in range(ATTN_HEADS):
            scores(2 * v + 2, hh, sa_ref, mxa_ref)
            update(2 * v + 1, hh, sb_ref, mxb_ref)
        return carry

    n_units = lax.shift_right_logical(i + (UNIT_BLOCKS - 1), UNIT_SHIFT)
    lax.fori_loop(0, lax.shift_right_logical(n_units + 1, 1), body, 0)
    for hh in range(ATTN_HEADS):
        o_ref[:, head_rows(hh)] = (acc_ref[hh] * (1.0 / l_ref[hh])).T.astype(o_ref.dtype)


def _attn(qt, k, vt, bias):
    s = k.shape[0]
    nb = s // MOBA_BLOCK
    assert nb <= QK_DEPTH - HEAD_DIM and nb % (2 * UNIT_BLOCKS) == 0 and nb <= UNIT_KEYS
    k3 = k.reshape(nb, MOBA_BLOCK, D_MODEL)
    gw = ATTN_HEADS * HEAD_DIM
    scores_buf = pltpu.VMEM((ATTN_HEADS, UNIT_KEYS, MOBA_BLOCK), _F32)
    row_buf = pltpu.VMEM((ATTN_HEADS, 1, MOBA_BLOCK), _F32)
    return pl.pallas_call(
        functools.partial(_attn_kernel, n_blocks=nb),
        grid=(N_HEADS // ATTN_HEADS, nb),
        in_specs=[
            pl.BlockSpec((gw, MOBA_BLOCK), lambda g, i: (g, i)),
            pl.BlockSpec((nb, MOBA_BLOCK, gw), lambda g, i: (0, 0, g), pipeline_mode=pl.Buffered(1)),
            pl.BlockSpec((ATTN_HEADS, nb, HEAD_DIM, MOBA_BLOCK), lambda g, i: (g, 0, 0, 0),
                         pipeline_mode=pl.Buffered(1)),
            pl.BlockSpec((ATTN_HEADS, nb, MOBA_BLOCK), lambda g, i: (g, 0, i)),
        ],
        out_specs=pl.BlockSpec((MOBA_BLOCK, gw), lambda g, i: (i, g)),
        out_shape=jax.ShapeDtypeStruct((s, D_MODEL), _BF),
        scratch_shapes=[
            pltpu.VMEM((ATTN_HEADS, QK_DEPTH, MOBA_BLOCK), _BF),
            scores_buf,
            scores_buf,
            row_buf,
            row_buf,
            row_buf,
            row_buf,
            pltpu.VMEM((ATTN_HEADS, HEAD_DIM, MOBA_BLOCK), _F32),
        ],
        compiler_params=pltpu.CompilerParams(
            dimension_semantics=("arbitrary", "arbitrary"), vmem_limit_bytes=VMEM_LIMIT),
        name="moba_attn",
    )(qt, k3, vt, bias)


def _post_kernel(o_ref, sa_ref, sgb_ref, x_ref, wob_ref, wo_ref, nf_ref, wup_ref, fcw_ref, fcb_ref,
                 wdn_ref, out_ref, halo_ref, ubuf_ref):
    i = pl.program_id(0)

    @pl.when(i == 0)
    def _():
        halo_ref[...] = jnp.zeros_like(halo_ref)

    branch_b = _dot(o_ref[...], wob_ref[...])
    merged = sa_ref[...] + sgb_ref[...] * branch_b
    h = x_ref[...] + _dot(merged.astype(_BF), wo_ref[...])
    hn = (_rms(h) * nf_ref[...]).astype(_BF)

    n_chunks = D_FF // FFN_CHUNK

    def up_proj(c):
        for part in range(2):
            lo = part * D_FF + c * FFN_CHUNK
            u = _dot(hn, wup_ref[:, lo:lo + FFN_CHUNK])
            buf = ubuf_ref.at[c % (FFN_LOOKAHEAD + 1), part]
            buf[:CONV_HALO_ROWS, :] = halo_ref[:, lo:lo + FFN_CHUNK]
            buf[CONV_HALO_ROWS:, :] = u
            halo_ref[:, lo:lo + FFN_CHUNK] = u[POST_ROWS - CONV_HALO_ROWS:, :]

    def conv(c, part):
        lo = part * D_FF + c * FFN_CHUNK
        buf = ubuf_ref.at[c % (FFN_LOOKAHEAD + 1), part]
        w = fcw_ref[:, lo:lo + FFN_CHUNK]
        taps = [buf[CONV_HALO_ROWS - 2 + j:CONV_HALO_ROWS - 2 + j + POST_ROWS, :] for j in range(3)]
        return (w[0:1, :] * taps[0] + w[1:2, :] * taps[1] + w[2:3, :] * taps[2]
                + fcb_ref[:, lo:lo + FFN_CHUNK])

    acc = h
    for c in range(min(FFN_LOOKAHEAD, n_chunks)):
        up_proj(c)
    for c in range(n_chunks):
        if c + FFN_LOOKAHEAD < n_chunks:
            up_proj(c + FFN_LOOKAHEAD)
        act = (jax.nn.silu(conv(c, 0)) * conv(c, 1)).astype(_BF)
        acc = acc + _dot(act, wdn_ref[c * FFN_CHUNK:(c + 1) * FFN_CHUNK, :])
    out_ref[...] = acc


def _post(o, sa, sgb, x, wob, wo, nf, wup, fcw, fcb, wdn):
    s = x.shape[0]
    const = lambda shape: pl.BlockSpec(shape, lambda i: (0,) * len(shape), pipeline_mode=pl.Buffered(1))
    rows = pl.BlockSpec((POST_ROWS, D_MODEL), lambda i: (i, 0))
    return pl.pallas_call(
        _post_kernel,
        grid=(s // POST_ROWS,),
        in_specs=[
            rows, rows, rows, rows,
            const((D_MODEL, D_MODEL)),
            const((D_MODEL, D_MODEL)),
            const((1, D_MODEL)),
            const((D_MODEL, 2 * D_FF)),
            const((3, 2 * D_FF)),
            const((1, 2 * D_FF)),
            const((D_FF, D_MODEL)),
        ],
        out_specs=rows,
        out_shape=jax.ShapeDtypeStruct((s, D_MODEL), _F32),
        scratch_shapes=[
            pltpu.VMEM((CONV_HALO_ROWS, 2 * D_FF), _F32),
            pltpu.VMEM((FFN_LOOKAHEAD + 1, 2, CONV_HALO_ROWS + POST_ROWS, FFN_CHUNK), _F32),
        ],
        compiler_params=pltpu.CompilerParams(
            dimension_semantics=("arbitrary",), vmem_limit_bytes=VMEM_LIMIT),
        name="post",
    )(o, sa, sgb, x, wob, wo, nf, wup, fcw, fcb, wdn)


def kernel(x, norm_mix, w_in, conv_a_w, q_norm, k_norm, w_out_a, w_out_b, w_o, norm_ffn, w_up,
           ffn_conv_w, ffn_conv_b, w_down):
    b, s, d = x.shape
    assert b == 1 and d == D_MODEL and s % POST_ROWS == 0 and s % MOBA_BLOCK == 0
    x2 = x.reshape(s, d)
    wb = w_in.astype(_BF)
    wc = wb[:, :3 * D_MODEL]
    wqt = wb[:, 3 * D_MODEL:4 * D_MODEL].T
    wk = wb[:, 4 * D_MODEL:5 * D_MODEL]
    wvt = wb[:, 5 * D_MODEL:6 * D_MODEL].T
    wg = wb[:, 6 * D_MODEL:]

    sa, sgb, qt, k, vt, bias = _proj(
        x2, norm_mix.reshape(1, d), wc, wqt, wk, wvt, wg, conv_a_w, q_norm.reshape(HEAD_DIM, 1),
        k_norm.reshape(1, HEAD_DIM), w_out_a.astype(_BF))
    o = _attn(qt, k, vt, bias)
    out = _post(o, sa, sgb, x2, w_out_b.astype(_BF), w_o.astype(_BF), norm_ffn.reshape(1, d),
                w_up.astype(_BF), ffn_conv_w, ffn_conv_b.reshape(1, 2 * D_FF), w_down.astype(_BF))
    return out.reshape(b, s, d)
```

```python
import functools

import jax
import jax.numpy as jnp
from jax import lax
from jax.experimental import pallas as pl
from jax.experimental.pallas import tpu as pltpu

D_MODEL = 1024
N_HEADS = 8
HEAD_DIM = 128
MOBA_BLOCK = 256
MOBA_TOPK = 3
D_FF = 2816
EPS = 1e-6
CONV_HALO_ROWS = 8
NEG_BIG = -1e30

PROJ_ROWS = 256
POST_ROWS = 512
FFN_CHUNK = 256
FFN_LOOKAHEAD = 2
VMEM_LIMIT = 56 * 1024 * 1024
ATTN_HEADS = 4
ONES_ROWS = 16
UNIT_SHIFT = 1
UNIT_BLOCKS = 1 << UNIT_SHIFT
UNIT_KEYS = UNIT_BLOCKS * MOBA_BLOCK
LONG_TRIP_SHIFT = 1
LONG_TRIP_PAIRS = 1 << LONG_TRIP_SHIFT
QK_DEPTH = 2 * HEAD_DIM
QK_SCALE = HEAD_DIM ** -0.5 * 1.4426950408889634

_BF = jnp.bfloat16
_F32 = jnp.float32


def _dot(a, b):
    return jnp.dot(a, b, preferred_element_type=_F32)


def _dot_nt(a, b):
    return lax.dot_general(a, b, (((1,), (1,)), ((), ())), preferred_element_type=_F32)


def _rms(x, axis=-1):
    return x * lax.rsqrt(jnp.mean(x * x, axis=axis, keepdims=True) + EPS)


def _causal_conv3(c, prev, w):
    row = lax.broadcasted_iota(jnp.int32, c.shape, 0)
    p1 = prev[CONV_HALO_ROWS - 1:CONV_HALO_ROWS, :]
    p2 = prev[CONV_HALO_ROWS - 2:CONV_HALO_ROWS - 1, :]
    c1 = jnp.where(row == 0, p1, pltpu.roll(c, 1, 0))
    c2 = jnp.where(row == 0, p2, jnp.where(row == 1, p1, pltpu.roll(c, 2, 0)))
    return w[0:1, :] * c2 + w[1:2, :] * c1 + w[2:3, :] * c


def _proj_kernel(x_ref, nm_ref, wc_ref, wqt_ref, wk_ref, wvt_ref, wg_ref, cw_ref, qn_ref, kn_ref, woa_ref,
                 sa_ref, sgb_ref, qt_ref, k_ref, vt_ref, bias_ref, halo_ref, kbar_ref):
    i = pl.program_id(0)

    @pl.when(i == 0)
    def _():
        halo_ref[...] = jnp.zeros_like(halo_ref)
        kbar_ref[...] = jnp.zeros_like(kbar_ref)

    xb = (_rms(x_ref[...]) * nm_ref[...]).astype(_BF)
    n_blocks = kbar_ref.shape[0]

    zk = _dot(xb, wk_ref[...])
    zqt = _dot_nt(wqt_ref[...], xb)

    gates = []
    for h in range(N_HEADS):
        lo = h * HEAD_DIM
        qh = (_rms(zqt[lo:lo + HEAD_DIM, :], axis=0) * qn_ref[...] * QK_SCALE).astype(_BF)
        qt_ref[lo:lo + HEAD_DIM, :] = qh
        gates.append(_dot(kbar_ref[:, lo:lo + HEAD_DIM].astype(_BF), qh))

    zc = _dot(xb, wc_ref[...])

    for h in range(N_HEADS):
        gate = gates[h]
        blk = lax.broadcasted_iota(jnp.int32, gate.shape, 0)
        gate = jnp.where(blk < i, gate, -jnp.inf)
        bias = jnp.full(gate.shape, NEG_BIG, _F32)
        for _ in range(MOBA_TOPK):
            top = jnp.max(gate, axis=0, keepdims=True)
            first = jnp.min(jnp.where(gate == top, blk, n_blocks), axis=0, keepdims=True)
            hit = blk == first
            bias = jnp.where(hit, 0.0, bias)
            gate = jnp.where(hit, -jnp.inf, gate)
        bias_ref[h] = jnp.where(blk < i, bias, NEG_BIG).astype(_BF)

    kbar_rows = []
    for h in range(N_HEADS):
        lo = h * HEAD_DIM
        kh = _rms(zk[:, lo:lo + HEAD_DIM]) * kn_ref[...]
        k_ref[:, lo:lo + HEAD_DIM] = kh.astype(_BF)
        kbar_rows.append(jnp.mean(kh, axis=0, keepdims=True))
    kbar_all = kbar_ref[...]
    blk_row = lax.broadcasted_iota(jnp.int32, kbar_all.shape, 0)
    kbar_ref[...] = jnp.where(blk_row == i, jnp.concatenate(kbar_rows, axis=1), kbar_all)

    zg = _dot(xb, wg_ref[...])

    bg = zc[:, :D_MODEL]
    c = zc[:, D_MODEL:2 * D_MODEL] * zc[:, 2 * D_MODEL:]
    y = _causal_conv3(c, halo_ref[...], cw_ref[...])
    halo_ref[...] = c[PROJ_ROWS - CONV_HALO_ROWS:, :]
    branch_a = _dot((bg * y).astype(_BF), woa_ref[...])

    vt = _dot_nt(wvt_ref[...], xb)

    sa_ref[...] = jax.nn.sigmoid(zg[:, :D_MODEL]) * branch_a
    sgb_ref[...] = jax.nn.sigmoid(zg[:, D_MODEL:])
    for h in range(N_HEADS):
        vt_ref[h, 0] = vt[h * HEAD_DIM:(h + 1) * HEAD_DIM, :].astype(_BF)


def _proj(x, nm, wc, wqt, wk, wvt, wg, cw, qn, kn, woa):
    s = x.shape[0]
    assert PROJ_ROWS == MOBA_BLOCK
    nb = s // PROJ_ROWS
    const = lambda shape: pl.BlockSpec(shape, lambda i: (0,) * len(shape), pipeline_mode=pl.Buffered(1))
    rows = lambda w: pl.BlockSpec((PROJ_ROWS, w), lambda i: (i, 0))
    return pl.pallas_call(
        _proj_kernel,
        grid=(nb,),
        in_specs=[
            rows(D_MODEL),
            const((1, D_MODEL)),
            const((D_MODEL, 3 * D_MODEL)),
            const((D_MODEL, D_MODEL)),
            const((D_MODEL, D_MODEL)),
            const((D_MODEL, D_MODEL)),
            const((D_MODEL, 2 * D_MODEL)),
            const((3, D_MODEL)),
            const((HEAD_DIM, 1)),
            const((1, HEAD_DIM)),
            const((D_MODEL, D_MODEL)),
        ],
        out_specs=[
            rows(D_MODEL),
            rows(D_MODEL),
            pl.BlockSpec((D_MODEL, PROJ_ROWS), lambda i: (0, i)),
            rows(D_MODEL),
            pl.BlockSpec((N_HEADS, 1, HEAD_DIM, MOBA_BLOCK), lambda i: (0, i, 0, 0)),
            pl.BlockSpec((N_HEADS, nb, PROJ_ROWS), lambda i: (0, 0, i)),
        ],
        out_shape=[
            jax.ShapeDtypeStruct((s, D_MODEL), _F32),
            jax.ShapeDtypeStruct((s, D_MODEL), _F32),
            jax.ShapeDtypeStruct((D_MODEL, s), _BF),
            jax.ShapeDtypeStruct((s, D_MODEL), _BF),
            jax.ShapeDtypeStruct((N_HEADS, nb, HEAD_DIM, MOBA_BLOCK), _BF),
            jax.ShapeDtypeStruct((N_HEADS, nb, s), _BF),
        ],
        scratch_shapes=[
            pltpu.VMEM((CONV_HALO_ROWS, D_MODEL), _F32),
            pltpu.VMEM((nb, D_MODEL), _F32),
        ],
        compiler_params=pltpu.CompilerParams(
            dimension_semantics=("arbitrary",), vmem_limit_bytes=VMEM_LIMIT),
        name="proj",
    )(x, nm, wc, wqt, wk, wvt, wg, cw, qn, kn, woa)


def _attn_kernel(qt_ref, k_ref, vt_ref, bias_ref, o_ref, qaug_ref, sa_ref, sb_ref, mxa_ref, mxb_ref,
                 m_ref, l_ref, acc_ref, *, n_blocks):
    i = pl.program_id(1)
    ones_rows = jnp.ones((ONES_ROWS, MOBA_BLOCK), _BF)
    pad_rows = jnp.zeros((QK_DEPTH - HEAD_DIM - n_blocks, MOBA_BLOCK), _BF)
    lane = lax.broadcasted_iota(jnp.int32, (ONES_ROWS, HEAD_DIM), 1)

    def head_rows(hh):
        return slice(hh * HEAD_DIM, (hh + 1) * HEAD_DIM)

    def unit_blocks(u):
        if isinstance(u, int) and u == 0:
            return [i] + list(range(UNIT_BLOCKS - 1))
        j0 = jnp.minimum(UNIT_BLOCKS * u - 1, n_blocks - UNIT_BLOCKS)
        return [j0 + b for b in range(UNIT_BLOCKS)]

    def keys_with_onehot(j, hh, bias_row):
        row = (lane == bias_row).astype(_BF)
        onehot = jnp.concatenate([row] * (MOBA_BLOCK // ONES_ROWS), axis=0)
        return jnp.concatenate([k_ref[j, :, head_rows(hh)], onehot], axis=1)

    def put_scores(hh, s, dst_ref, mx_ref):
        dst_ref[hh] = s
        mx_ref[hh] = jnp.max(s, axis=0, keepdims=True)

    def scores(u, hh, dst_ref, mx_ref):
        ka = jnp.concatenate([keys_with_onehot(j, hh, j) for j in unit_blocks(u)], axis=0)
        put_scores(hh, _dot(ka, qaug_ref[hh]), dst_ref, mx_ref)

    def first_scores(hh, dst_ref, mx_ref):
        js = unit_blocks(0)
        ka = jnp.concatenate([keys_with_onehot(js[0], hh, n_blocks)]
                             + [keys_with_onehot(j, hh, j) for j in js[1:]], axis=0)
        s = _dot(ka, qaug_ref[hh])
        own = s[:MOBA_BLOCK]
        kpos = lax.broadcasted_iota(jnp.int32, own.shape, 0)
        qpos = lax.broadcasted_iota(jnp.int32, own.shape, 1)
        own = jnp.where(kpos <= qpos, own, NEG_BIG)
        put_scores(hh, jnp.concatenate([own, s[MOBA_BLOCK:]], axis=0), dst_ref, mx_ref)

    def update(u, hh, src_ref, mx_ref):
        m_old = m_ref[hh]
        m_new = jnp.maximum(m_old, mx_ref[hh])
        alpha = jnp.exp2(m_old - m_new)
        outs = []
        for b, j in enumerate(unit_blocks(u)):
            p = jnp.exp2(src_ref[hh, b * MOBA_BLOCK:(b + 1) * MOBA_BLOCK, :] - m_new).astype(_BF)
            outs.append(_dot(jnp.concatenate([vt_ref[hh, j], ones_rows], axis=0), p))
        out = functools.reduce(jnp.add, outs)
        m_ref[hh] = m_new
        l_ref[hh] = alpha * l_ref[hh] + out[HEAD_DIM:HEAD_DIM + 1]
        acc_ref[hh] = alpha * acc_ref[hh] + out[:HEAD_DIM]

    for hh in range(ATTN_HEADS):
        qaug_ref[hh] = jnp.concatenate([qt_ref[head_rows(hh), :], bias_ref[hh], pad_rows], axis=0)
        m_ref[hh] = jnp.full(m_ref.shape[1:], NEG_BIG, _F32)
        l_ref[hh] = jnp.zeros(l_ref.shape[1:], _F32)
        acc_ref[hh] = jnp.zeros(acc_ref.shape[1:], _F32)
    for hh in range(ATTN_HEADS):
        first_scores(hh, sa_ref, mxa_ref)

    def unit_pairs(u0, n_pairs):
        for pair in range(n_pairs):
            u = u0 + 2 * pair
            for hh in range(ATTN_HEADS):
                scores(u + 1, hh, sb_ref, mxb_ref)
                update(u, hh, sa_ref, mxa_ref)
            for hh in range(ATTN_HEADS):
                scores(u + 2, hh, sa_ref, mxa_ref)
                update(u + 1, hh, sb_ref, mxb_ref)

    unit_pairs(0, 1)
    n_units = lax.shift_right_logical(i + UNIT_BLOCKS, UNIT_SHIFT)
    more_pairs = lax.shift_right_logical(n_units + 1, 1) - 1
    long_trips = lax.shift_right_logical(more_pairs, LONG_TRIP_SHIFT)

    def long_body(t, carry):
        unit_pairs(2 + 2 * LONG_TRIP_PAIRS * t, LONG_TRIP_PAIRS)
        return carry

    def short_body(t, carry):
        unit_pairs(2 + 2 * (LONG_TRIP_PAIRS * long_trips + t), 1)
        return carry

    lax.fori_loop(0, long_trips, long_body, 0)
    lax.fori_loop(0, more_pairs - LONG_TRIP_PAIRS * long_trips, short_body, 0)
    for hh in range(ATTN_HEADS):
        o_ref[:, head_rows(hh)] = (acc_ref[hh] * (1.0 / l_ref[hh])).T.astype(o_ref.dtype)


def _attn(qt, k, vt, bias):
    s = k.shape[0]
    nb = s // MOBA_BLOCK
    assert nb < QK_DEPTH - HEAD_DIM and nb % (2 * UNIT_BLOCKS) == 0
    k3 = k.reshape(nb, MOBA_BLOCK, D_MODEL)
    gw = ATTN_HEADS * HEAD_DIM
    scores_buf = pltpu.VMEM((ATTN_HEADS, UNIT_KEYS, MOBA_BLOCK), _F32)
    row_buf = pltpu.VMEM((ATTN_HEADS, 1, MOBA_BLOCK), _F32)
    return pl.pallas_call(
        functools.partial(_attn_kernel, n_blocks=nb),
        grid=(N_HEADS // ATTN_HEADS, nb),
        in_specs=[
            pl.BlockSpec((gw, MOBA_BLOCK), lambda g, i: (g, i)),
            pl.BlockSpec((nb, MOBA_BLOCK, gw), lambda g, i: (0, 0, g), pipeline_mode=pl.Buffered(1)),
            pl.BlockSpec((ATTN_HEADS, nb, HEAD_DIM, MOBA_BLOCK), lambda g, i: (g, 0, 0, 0),
                         pipeline_mode=pl.Buffered(1)),
            pl.BlockSpec((ATTN_HEADS, nb, MOBA_BLOCK), lambda g, i: (g, 0, i)),
        ],
        out_specs=pl.BlockSpec((MOBA_BLOCK, gw), lambda g, i: (i, g)),
        out_shape=jax.ShapeDtypeStruct((s, D_MODEL), _BF),
        scratch_shapes=[
            pltpu.VMEM((ATTN_HEADS, QK_DEPTH, MOBA_BLOCK), _BF),
            scores_buf,
            scores_buf,
            row_buf,
            row_buf,
            row_buf,
            row_buf,
            pltpu.VMEM((ATTN_HEADS, HEAD_DIM, MOBA_BLOCK), _F32),
        ],
        compiler_params=pltpu.CompilerParams(
            dimension_semantics=("arbitrary", "arbitrary"), vmem_limit_bytes=VMEM_LIMIT),
        name="moba_attn",
    )(qt, k3, vt, bias)


def _post_kernel(o_ref, sa_ref, sgb_ref, x_ref, wob_ref, wo_ref, nf_ref, wup_ref, fcw_ref, fcb_ref,
                 wdn_ref, out_ref, halo_ref, ubuf_ref):
    i = pl.program_id(0)

    @pl.when(i == 0)
    def _():
        halo_ref[...] = jnp.zeros_like(halo_ref)

    branch_b = _dot(o_ref[...], wob_ref[...])
    merged = sa_ref[...] + sgb_ref[...] * branch_b
    h = x_ref[...] + _dot(merged.astype(_BF), wo_ref[...])
    hn = (_rms(h) * nf_ref[...]).astype(_BF)

    n_chunks = D_FF // FFN_CHUNK

    def up_proj(c):
        for part in range(2):
            lo = part * D_FF + c * FFN_CHUNK
            u = _dot(hn, wup_ref[:, lo:lo + FFN_CHUNK])
            buf = ubuf_ref.at[c % (FFN_LOOKAHEAD + 1), part]
            buf[:CONV_HALO_ROWS, :] = halo_ref[:, lo:lo + FFN_CHUNK]
            buf[CONV_HALO_ROWS:, :] = u
            halo_ref[:, lo:lo + FFN_CHUNK] = u[POST_ROWS - CONV_HALO_ROWS:, :]

    def conv(c, part):
        lo = part * D_FF + c * FFN_CHUNK
        buf = ubuf_ref.at[c % (FFN_LOOKAHEAD + 1), part]
        w = fcw_ref[:, lo:lo + FFN_CHUNK]
        taps = [buf[CONV_HALO_ROWS - 2 + j:CONV_HALO_ROWS - 2 + j + POST_ROWS, :] for j in range(3)]
        return (w[0:1, :] * taps[0] + w[1:2, :] * taps[1] + w[2:3, :] * taps[2]
                + fcb_ref[:, lo:lo + FFN_CHUNK])

    acc = h
    for c in range(min(FFN_LOOKAHEAD, n_chunks)):
        up_proj(c)
    for c in range(n_chunks):
        if c + FFN_LOOKAHEAD < n_chunks:
            up_proj(c + FFN_LOOKAHEAD)
        act = (jax.nn.silu(conv(c, 0)) * conv(c, 1)).astype(_BF)
        acc = acc + _dot(act, wdn_ref[c * FFN_CHUNK:(c + 1) * FFN_CHUNK, :])
    out_ref[...] = acc


def _post(o, sa, sgb, x, wob, wo, nf, wup, fcw, fcb, wdn):
    s = x.shape[0]
    const = lambda shape: pl.BlockSpec(shape, lambda i: (0,) * len(shape), pipeline_mode=pl.Buffered(1))
    rows = pl.BlockSpec((POST_ROWS, D_MODEL), lambda i: (i, 0))
    return pl.pallas_call(
        _post_kernel,
        grid=(s // POST_ROWS,),
        in_specs=[
            rows, rows, rows, rows,
            const((D_MODEL, D_MODEL)),
            const((D_MODEL, D_MODEL)),
            const((1, D_MODEL)),
            const((D_MODEL, 2 * D_FF)),
            const((3, 2 * D_FF)),
            const((1, 2 * D_FF)),
            const((D_FF, D_MODEL)),
        ],
        out_specs=rows,
        out_shape=jax.ShapeDtypeStruct((s, D_MODEL), _F32),
        scratch_shapes=[
            pltpu.VMEM((CONV_HALO_ROWS, 2 * D_FF), _F32),
            pltpu.VMEM((FFN_LOOKAHEAD + 1, 2, CONV_HALO_ROWS + POST_ROWS, FFN_CHUNK), _F32),
        ],
        compiler_params=pltpu.CompilerParams(
            dimension_semantics=("arbitrary",), vmem_limit_bytes=VMEM_LIMIT),
        name="post",
    )(o, sa, sgb, x, wob, wo, nf, wup, fcw, fcb, wdn)


def kernel(x, norm_mix, w_in, conv_a_w, q_norm, k_norm, w_out_a, w_out_b, w_o, norm_ffn, w_up,
           ffn_conv_w, ffn_conv_b, w_down):
    b, s, d = x.shape
    assert b == 1 and d == D_MODEL and s % POST_ROWS == 0 and s % MOBA_BLOCK == 0
    x2 = x.reshape(s, d)
    wb = w_in.astype(_BF)
    wc = wb[:, :3 * D_MODEL]
    wqt = wb[:, 3 * D_MODEL:4 * D_MODEL].T
    wk = wb[:, 4 * D_MODEL:5 * D_MODEL]
    wvt = wb[:, 5 * D_MODEL:6 * D_MODEL].T
    wg = wb[:, 6 * D_MODEL:]

    sa, sgb, qt, k, vt, bias = _proj(
        x2, norm_mix.reshape(1, d), wc, wqt, wk, wvt, wg, conv_a_w, q_norm.reshape(HEAD_DIM, 1),
        k_norm.reshape(1, HEAD_DIM), w_out_a.astype(_BF))
    o = _attn(qt, k, vt, bias)
    out = _post(o, sa, sgb, x2, w_out_b.astype(_BF), w_o.astype(_BF), norm_ffn.reshape(1, d),
                w_up.astype(_BF), ffn_conv_w, ffn_conv_b.reshape(1, 2 * D_FF), w_down.astype(_BF))
    return out.reshape(b, s, d)
```

```python
import functools

import jax
import jax.numpy as jnp
from jax import lax
from jax.experimental import pallas as pl
from jax.experimental.pallas import tpu as pltpu

D_MODEL = 1024
N_HEADS = 8
HEAD_DIM = 128
MOBA_BLOCK = 256
MOBA_TOPK = 3
D_FF = 2816
EPS = 1e-6
CONV_HALO_ROWS = 8
NEG_BIG = -1e30

PROJ_ROWS = 256
POST_ROWS = 512
FFN_CHUNK = 256
FFN_LOOKAHEAD = 3
VMEM_LIMIT = 56 * 1024 * 1024
ATTN_HEADS = 4
ONES_ROWS = 16
UNIT_SHIFT = 1
UNIT_BLOCKS = 1 << UNIT_SHIFT
UNIT_KEYS = UNIT_BLOCKS * MOBA_BLOCK
LONG_TRIP_SHIFT = 1
LONG_TRIP_PAIRS = 1 << LONG_TRIP_SHIFT
QK_DEPTH = 2 * HEAD_DIM
QK_SCALE = HEAD_DIM ** -0.5 * 1.4426950408889634

_BF = jnp.bfloat16
_F32 = jnp.float32


def _dot(a, b):
    return jnp.dot(a, b, preferred_element_type=_F32)


def _dot_nt(a, b):
    return lax.dot_general(a, b, (((1,), (1,)), ((), ())), preferred_element_type=_F32)


def _rms(x, axis=-1):
    return x * lax.rsqrt(jnp.mean(x * x, axis=axis, keepdims=True) + EPS)


def _causal_conv3(c, prev, w):
    row = lax.broadcasted_iota(jnp.int32, c.shape, 0)
    p1 = prev[CONV_HALO_ROWS - 1:CONV_HALO_ROWS, :]
    p2 = prev[CONV_HALO_ROWS - 2:CONV_HALO_ROWS - 1, :]
    c1 = jnp.where(row == 0, p1, pltpu.roll(c, 1, 0))
    c2 = jnp.where(row == 0, p2, jnp.where(row == 1, p1, pltpu.roll(c, 2, 0)))
    return w[0:1, :] * c2 + w[1:2, :] * c1 + w[2:3, :] * c


def _proj_kernel(x_ref, nm_ref, wc_ref, wqt_ref, wk_ref, wvt_ref, wg_ref, cw_ref, qn_ref, kn_ref, woa_ref,
                 sa_ref, sgb_ref, qt_ref, k_ref, vt_ref, bias_ref, halo_ref, kbar_ref):
    i = pl.program_id(0)

    @pl.when(i == 0)
    def _():
        halo_ref[...] = jnp.zeros_like(halo_ref)
        kbar_ref[...] = jnp.zeros_like(kbar_ref)

    xb = (_rms(x_ref[...]) * nm_ref[...]).astype(_BF)
    n_blocks = kbar_ref.shape[0]

    zk = _dot(xb, wk_ref[...])
    zqt = _dot_nt(wqt_ref[...], xb)

    gates = []
    for h in range(N_HEADS):
        lo = h * HEAD_DIM
        qh = (_rms(zqt[lo:lo + HEAD_DIM, :], axis=0) * qn_ref[...] * QK_SCALE).astype(_BF)
        qt_ref[lo:lo + HEAD_DIM, :] = qh
        gates.append(_dot(kbar_ref[:, lo:lo + HEAD_DIM].astype(_BF), qh))

    zc = _dot(xb, wc_ref[...])

    for h in range(N_HEADS):
        gate = gates[h]
        blk = lax.broadcasted_iota(jnp.int32, gate.shape, 0)
        gate = jnp.where(blk < i, gate, -jnp.inf)
        bias = jnp.full(gate.shape, NEG_BIG, _F32)
        for _ in range(MOBA_TOPK):
            top = jnp.max(gate, axis=0, keepdims=True)
            first = jnp.min(jnp.where(gate == top, blk, n_blocks), axis=0, keepdims=True)
            hit = blk == first
            bias = jnp.where(hit, 0.0, bias)
            gate = jnp.where(hit, -jnp.inf, gate)
        bias_ref[h] = jnp.where(blk < i, bias, NEG_BIG).astype(_BF)

    kbar_rows = []
    for h in range(N_HEADS):
        lo = h * HEAD_DIM
        kh = _rms(zk[:, lo:lo + HEAD_DIM]) * kn_ref[...]
        k_ref[:, lo:lo + HEAD_DIM] = kh.astype(_BF)
        kbar_rows.append(jnp.mean(kh, axis=0, keepdims=True))
    kbar_all = kbar_ref[...]
    blk_row = lax.broadcasted_iota(jnp.int32, kbar_all.shape, 0)
    kbar_ref[...] = jnp.where(blk_row == i, jnp.concatenate(kbar_rows, axis=1), kbar_all)

    zg = _dot(xb, wg_ref[...])

    bg = zc[:, :D_MODEL]
    c = zc[:, D_MODEL:2 * D_MODEL] * zc[:, 2 * D_MODEL:]
    y = _causal_conv3(c, halo_ref[...], cw_ref[...])
    halo_ref[...] = c[PROJ_ROWS - CONV_HALO_ROWS:, :]
    branch_a = _dot((bg * y).astype(_BF), woa_ref[...])

    vt = _dot_nt(wvt_ref[...], xb)

    sa_ref[...] = jax.nn.sigmoid(zg[:, :D_MODEL]) * branch_a
    sgb_ref[...] = jax.nn.sigmoid(zg[:, D_MODEL:])
    for h in range(N_HEADS):
        vt_ref[h, 0] = vt[h * HEAD_DIM:(h + 1) * HEAD_DIM, :].astype(_BF)


def _proj(x, nm, wc, wqt, wk, wvt, wg, cw, qn, kn, woa):
    s = x.shape[0]
    assert PROJ_ROWS == MOBA_BLOCK
    nb = s // PROJ_ROWS
    const = lambda shape: pl.BlockSpec(shape, lambda i: (0,) * len(shape), pipeline_mode=pl.Buffered(1))
    rows = lambda w: pl.BlockSpec((PROJ_ROWS, w), lambda i: (i, 0))
    return pl.pallas_call(
        _proj_kernel,
        grid=(nb,),
        in_specs=[
            rows(D_MODEL),
            const((1, D_MODEL)),
            const((D_MODEL, 3 * D_MODEL)),
            const((D_MODEL, D_MODEL)),
            const((D_MODEL, D_MODEL)),
            const((D_MODEL, D_MODEL)),
            const((D_MODEL, 2 * D_MODEL)),
            const((3, D_MODEL)),
            const((HEAD_DIM, 1)),
            const((1, HEAD_DIM)),
            const((D_MODEL, D_MODEL)),
        ],
        out_specs=[
            rows(D_MODEL),
            rows(D_MODEL),
            pl.BlockSpec((D_MODEL, PROJ_ROWS), lambda i: (0, i)),
            rows(D_MODEL),
            pl.BlockSpec((N_HEADS, 1, HEAD_DIM, MOBA_BLOCK), lambda i: (0, i, 0, 0)),
            pl.BlockSpec((N_HEADS, nb, PROJ_ROWS), lambda i: (0, 0, i)),
        ],
        out_shape=[
            jax.ShapeDtypeStruct((s, D_MODEL), _F32),
            jax.ShapeDtypeStruct((s, D_MODEL), _F32),
            jax.ShapeDtypeStruct((D_MODEL, s), _BF),
            jax.ShapeDtypeStruct((s, D_MODEL), _BF),
            jax.ShapeDtypeStruct((N_HEADS, nb, HEAD_DIM, MOBA_BLOCK), _BF),
            jax.ShapeDtypeStruct((N_HEADS, nb, s), _BF),
        ],
        scratch_shapes=[
            pltpu.VMEM((CONV_HALO_ROWS, D_MODEL), _F32),
            pltpu.VMEM((nb, D_MODEL), _F32),
        ],
        compiler_params=pltpu.CompilerParams(
            dimension_semantics=("arbitrary",), vmem_limit_bytes=VMEM_LIMIT),
        name="proj",
    )(x, nm, wc, wqt, wk, wvt, wg, cw, qn, kn, woa)


def _attn_kernel(qt_ref, k_ref, vt_ref, bias_ref, o_ref, qaug_ref, sa_ref, sb_ref, mxa_ref, mxb_ref,
                 m_ref, l_ref, acc_ref, *, n_blocks):
    i = pl.program_id(1)
    ones_rows = jnp.ones((ONES_ROWS, MOBA_BLOCK), _BF)
    pad_rows = jnp.zeros((QK_DEPTH - HEAD_DIM - n_blocks, MOBA_BLOCK), _BF)
    lane = lax.broadcasted_iota(jnp.int32, (ONES_ROWS, HEAD_DIM), 1)

    def head_rows(hh):
        return slice(hh * HEAD_DIM, (hh + 1) * HEAD_DIM)

    def unit_blocks(u):
        if isinstance(u, int) and u == 0:
            return [i] + list(range(UNIT_BLOCKS - 1))
        j0 = jnp.minimum(UNIT_BLOCKS * u - 1, n_blocks - UNIT_BLOCKS)
        return [j0 + b for b in range(UNIT_BLOCKS)]

    def keys_with_onehot(j, hh, bias_row):
        row = (lane == bias_row).astype(_BF)
        onehot = jnp.concatenate([row] * (MOBA_BLOCK // ONES_ROWS), axis=0)
        return jnp.concatenate([k_ref[j, :, head_rows(hh)], onehot], axis=1)

    def put_scores(hh, s, dst_ref, mx_ref):
        dst_ref[hh] = s
        mx_ref[hh] = jnp.max(s, axis=0, keepdims=True)

    def scores(u, hh, dst_ref, mx_ref):
        ka = jnp.concatenate([keys_with_onehot(j, hh, j) for j in unit_blocks(u)], axis=0)
        put_scores(hh, _dot(ka, qaug_ref[hh]), dst_ref, mx_ref)

    def first_scores(hh, dst_ref, mx_ref):
        js = unit_blocks(0)
        ka = jnp.concatenate([keys_with_onehot(js[0], hh, n_blocks)]
                             + [keys_with_onehot(j, hh, j) for j in js[1:]], axis=0)
        s = _dot(ka, qaug_ref[hh])
        own = s[:MOBA_BLOCK]
        kpos = lax.broadcasted_iota(jnp.int32, own.shape, 0)
        qpos = lax.broadcasted_iota(jnp.int32, own.shape, 1)
        own = jnp.where(kpos <= qpos, own, NEG_BIG)
        put_scores(hh, jnp.concatenate([own, s[MOBA_BLOCK:]], axis=0), dst_ref, mx_ref)

    def update(u, hh, src_ref, mx_ref):
        m_old = m_ref[hh]
        m_new = jnp.maximum(m_old, mx_ref[hh])
        alpha = jnp.exp2(m_old - m_new)
        outs = []
        for b, j in enumerate(unit_blocks(u)):
            p = jnp.exp2(src_ref[hh, b * MOBA_BLOCK:(b + 1) * MOBA_BLOCK, :] - m_new).astype(_BF)
            outs.append(_dot(jnp.concatenate([vt_ref[hh, j], ones_rows], axis=0), p))
        out = functools.reduce(jnp.add, outs)
        m_ref[hh] = m_new
        l_ref[hh] = alpha * l_ref[hh] + out[HEAD_DIM:HEAD_DIM + 1]
        acc_ref[hh] = alpha * acc_ref[hh] + out[:HEAD_DIM]

    for hh in range(ATTN_HEADS):
        qaug_ref[hh] = jnp.concatenate([qt_ref[head_rows(hh), :], bias_ref[hh], pad_rows], axis=0)
        m_ref[hh] = jnp.full(m_ref.shape[1:], NEG_BIG, _F32)
        l_ref[hh] = jnp.zeros(l_ref.shape[1:], _F32)
        acc_ref[hh] = jnp.zeros(acc_ref.shape[1:], _F32)
    for hh in range(ATTN_HEADS):
        first_scores(hh, sa_ref, mxa_ref)

    def unit_pairs(u0, n_pairs):
        for pair in range(n_pairs):
            u = u0 + 2 * pair
            for hh in range(ATTN_HEADS):
                scores(u + 1, hh, sb_ref, mxb_ref)
                update(u, hh, sa_ref, mxa_ref)
            for hh in range(ATTN_HEADS):
                scores(u + 2, hh, sa_ref, mxa_ref)
                update(u + 1, hh, sb_ref, mxb_ref)

    unit_pairs(0, 1)
    n_units = lax.shift_right_logical(i + UNIT_BLOCKS, UNIT_SHIFT)
    more_pairs = lax.shift_right_logical(n_units + 1, 1) - 1
    long_trips = lax.shift_right_logical(more_pairs, LONG_TRIP_SHIFT)

    def long_body(t, carry):
        unit_pairs(2 + 2 * LONG_TRIP_PAIRS * t, LONG_TRIP_PAIRS)
        return carry

    def short_body(t, carry):
        unit_pairs(2 + 2 * (LONG_TRIP_PAIRS * long_trips + t), 1)
        return carry

    lax.fori_loop(0, long_trips, long_body, 0)
    lax.fori_loop(0, more_pairs - LONG_TRIP_PAIRS * long_trips, short_body, 0)
    for hh in range(ATTN_HEADS):
        o_ref[:, head_rows(hh)] = (acc_ref[hh] * (1.0 / l_ref[hh])).T.astype(o_ref.dtype)


def _attn(qt, k, vt, bias):
    s = k.shape[0]
    nb = s // MOBA_BLOCK
    assert nb < QK_DEPTH - HEAD_DIM and nb % (2 * UNIT_BLOCKS) == 0
    k3 = k.reshape(nb, MOBA_BLOCK, D_MODEL)
    gw = ATTN_HEADS * HEAD_DIM
    scores_buf = pltpu.VMEM((ATTN_HEADS, UNIT_KEYS, MOBA_BLOCK), _F32)
    row_buf = pltpu.VMEM((ATTN_HEADS, 1, MOBA_BLOCK), _F32)
    return pl.pallas_call(
        functools.partial(_attn_kernel, n_blocks=nb),
        grid=(N_HEADS // ATTN_HEADS, nb),
        in_specs=[
            pl.BlockSpec((gw, MOBA_BLOCK), lambda g, i: (g, i)),
            pl.BlockSpec((nb, MOBA_BLOCK, gw), lambda g, i: (0, 0, g), pipeline_mode=pl.Buffered(1)),
            pl.BlockSpec((ATTN_HEADS, nb, HEAD_DIM, MOBA_BLOCK), lambda g, i: (g, 0, 0, 0),
                         pipeline_mode=pl.Buffered(1)),
            pl.BlockSpec((ATTN_HEADS, nb, MOBA_BLOCK), lambda g, i: (g, 0, i)),
        ],
        out_specs=pl.BlockSpec((MOBA_BLOCK, gw), lambda g, i: (i, g)),
        out_shape=jax.ShapeDtypeStruct((s, D_MODEL), _BF),
        scratch_shapes=[
            pltpu.VMEM((ATTN_HEADS, QK_DEPTH, MOBA_BLOCK), _BF),
            scores_buf,
            scores_buf,
            row_buf,
            row_buf,
            row_buf,
            row_buf,
            pltpu.VMEM((ATTN_HEADS, HEAD_DIM, MOBA_BLOCK), _F32),
        ],
        compiler_params=pltpu.CompilerParams(
            dimension_semantics=("arbitrary", "arbitrary"), vmem_limit_bytes=VMEM_LIMIT),
        name="moba_attn",
    )(qt, k3, vt, bias)


def _post_kernel(o_ref, sa_ref, sgb_ref, x_ref, wob_ref, wo_ref, nf_ref, wup_ref, fcw_ref, fcb_ref,
                 wdn_ref, out_ref, halo_ref, ubuf_ref):
    i = pl.program_id(0)

    @pl.when(i == 0)
    def _():
        halo_ref[...] = jnp.zeros_like(halo_ref)

    branch_b = _dot(o_ref[...], wob_ref[...])
    merged = sa_ref[...] + sgb_ref[...] * branch_b
    h = x_ref[...] + _dot(merged.astype(_BF), wo_ref[...])
    hn = (_rms(h) * nf_ref[...]).astype(_BF)

    n_chunks = D_FF // FFN_CHUNK

    def up_proj(c):
        for part in range(2):
            lo = part * D_FF + c * FFN_CHUNK
            u = _dot(hn, wup_ref[:, lo:lo + FFN_CHUNK])
            buf = ubuf_ref.at[c % (FFN_LOOKAHEAD + 1), part]
            buf[:CONV_HALO_ROWS, :] = halo_ref[:, lo:lo + FFN_CHUNK]
            buf[CONV_HALO_ROWS:, :] = u
            halo_ref[:, lo:lo + FFN_CHUNK] = u[POST_ROWS - CONV_HALO_ROWS:, :]

    def conv(c, part):
        lo = part * D_FF + c * FFN_CHUNK
        buf = ubuf_ref.at[c % (FFN_LOOKAHEAD + 1), part]
        w = fcw_ref[:, lo:lo + FFN_CHUNK]
        taps = [buf[CONV_HALO_ROWS - 2 + j:CONV_HALO_ROWS - 2 + j + POST_ROWS, :] for j in range(3)]
        return (w[0:1, :] * taps[0] + w[1:2, :] * taps[1] + w[2:3, :] * taps[2]
                + fcb_ref[:, lo:lo + FFN_CHUNK])

    acc = h
    for c in range(min(FFN_LOOKAHEAD, n_chunks)):
        up_proj(c)
    for c in range(n_chunks):
        if c + FFN_LOOKAHEAD < n_chunks:
            up_proj(c + FFN_LOOKAHEAD)
        act = (jax.nn.silu(conv(c, 0)) * conv(c, 1)).astype(_BF)
        acc = acc + _dot(act, wdn_ref[c * FFN_CHUNK:(c + 1) * FFN_CHUNK, :])
    out_ref[...] = acc


def _post(o, sa, sgb, x, wob, wo, nf, wup, fcw, fcb, wdn):
    s = x.shape[0]
    const = lambda shape: pl.BlockSpec(shape, lambda i: (0,) * len(shape), pipeline_mode=pl.Buffered(1))
    rows = pl.BlockSpec((POST_ROWS, D_MODEL), lambda i: (i, 0))
    return pl.pallas_call(
        _post_kernel,
        grid=(s // POST_ROWS,),
        in_specs=[
            rows, rows, rows, rows,
            const((D_MODEL, D_MODEL)),
            const((D_MODEL, D_MODEL)),
            const((1, D_MODEL)),
            const((D_MODEL, 2 * D_FF)),
            const((3, 2 * D_FF)),
            const((1, 2 * D_FF)),
            const((D_FF, D_MODEL)),
        ],
        out_specs=rows,
        out_shape=jax.ShapeDtypeStruct((s, D_MODEL), _F32),
        scratch_shapes=[
            pltpu.VMEM((CONV_HALO_ROWS, 2 * D_FF), _F32),
            pltpu.VMEM((FFN_LOOKAHEAD + 1, 2, CONV_HALO_ROWS + POST_ROWS, FFN_CHUNK), _F32),
        ],
        compiler_params=pltpu.CompilerParams(
            dimension_semantics=("arbitrary",), vmem_limit_bytes=VMEM_LIMIT),
        name="post",
    )(o, sa, sgb, x, wob, wo, nf, wup, fcw, fcb, wdn)


def kernel(x, norm_mix, w_in, conv_a_w, q_norm, k_norm, w_out_a, w_out_b, w_o, norm_ffn, w_up,
           ffn_conv_w, ffn_conv_b, w_down):
    b, s, d = x.shape
    assert b == 1 and d == D_MODEL and s % POST_ROWS == 0 and s % MOBA_BLOCK == 0
    x2 = x.reshape(s, d)
    wb = w_in.astype(_BF)
    wc = wb[:, :3 * D_MODEL]
    wqt = wb[:, 3 * D_MODEL:4 * D_MODEL].T
    wk = wb[:, 4 * D_MODEL:5 * D_MODEL]
    wvt = wb[:, 5 * D_MODEL:6 * D_MODEL].T
    wg = wb[:, 6 * D_MODEL:]

    sa, sgb, qt, k, vt, bias = _proj(
        x2, norm_mix.reshape(1, d), wc, wqt, wk, wvt, wg, conv_a_w, q_norm.reshape(HEAD_DIM, 1),
        k_norm.reshape(1, HEAD_DIM), w_out_a.astype(_BF))
    o = _attn(qt, k, vt, bias)
    out = _post(o, sa, sgb, x2, w_out_b.astype(_BF), w_o.astype(_BF), norm_ffn.reshape(1, d),
                w_up.astype(_BF), ffn_conv_w, ffn_conv_b.reshape(1, 2 * D_FF), w_down.astype(_BF))
    return out.reshape(b, s, d)
```

```python
import functools

import jax
import jax.numpy as jnp
from jax import lax
from jax.experimental import pallas as pl
from jax.experimental.pallas import tpu as pltpu

D_MODEL = 1024
N_HEADS = 8
HEAD_DIM = 128
MOBA_BLOCK = 256
MOBA_TOPK = 3
D_FF = 2816
EPS = 1e-6
CONV_HALO_ROWS = 8
NEG_BIG = -1e30

PROJ_ROWS = 256
POST_ROWS = 512
FFN_CHUNK = 256
FFN_LOOKAHEAD = 3
VMEM_LIMIT = 56 * 1024 * 1024
ATTN_HEADS = 4
ONES_ROWS = 16
UNIT_SHIFT = 1
UNIT_BLOCKS = 1 << UNIT_SHIFT
UNIT_KEYS = UNIT_BLOCKS * MOBA_BLOCK
LONG_TRIP_SHIFT = 1
LONG_TRIP_PAIRS = 1 << LONG_TRIP_SHIFT
QK_DEPTH = 2 * HEAD_DIM
QK_SCALE = HEAD_DIM ** -0.5 * 1.4426950408889634

_BF = jnp.bfloat16
_F32 = jnp.float32


def _dot(a, b):
    return jnp.dot(a, b, preferred_element_type=_F32)


def _dot_nt(a, b):
    return lax.dot_general(a, b, (((1,), (1,)), ((), ())), preferred_element_type=_F32)


def _rms(x, axis=-1):
    return x * lax.rsqrt(jnp.mean(x * x, axis=axis, keepdims=True) + EPS)


def _causal_conv3(c, prev, w):
    row = lax.broadcasted_iota(jnp.int32, c.shape, 0)
    p1 = prev[CONV_HALO_ROWS - 1:CONV_HALO_ROWS, :]
    p2 = prev[CONV_HALO_ROWS - 2:CONV_HALO_ROWS - 1, :]
    c1 = jnp.where(row == 0, p1, pltpu.roll(c, 1, 0))
    c2 = jnp.where(row == 0, p2, jnp.where(row == 1, p1, pltpu.roll(c, 2, 0)))
    return w[0:1, :] * c2 + w[1:2, :] * c1 + w[2:3, :] * c


def _proj_kernel(x_ref, nm_ref, wc_ref, wqt_ref, wk_ref, wvt_ref, wg_ref, cw_ref, qn_ref, kn_ref, woa_ref,
                 sa_ref, sgb_ref, qt_ref, k_ref, vt_ref, bias_ref, halo_ref, kbar_ref):
    i = pl.program_id(0)

    @pl.when(i == 0)
    def _():
        halo_ref[...] = jnp.zeros_like(halo_ref)
        kbar_ref[...] = jnp.zeros_like(kbar_ref)

    xb = (_rms(x_ref[...]) * nm_ref[...]).astype(_BF)
    n_blocks = kbar_ref.shape[0]

    zk = _dot(xb, wk_ref[...])
    zqt = _dot_nt(wqt_ref[...], xb)
    zc = _dot(xb, wc_ref[...])

    gates = []
    for h in range(N_HEADS):
        lo = h * HEAD_DIM
        qh = (_rms(zqt[lo:lo + HEAD_DIM, :], axis=0) * qn_ref[...] * QK_SCALE).astype(_BF)
        qt_ref[lo:lo + HEAD_DIM, :] = qh
        gates.append(_dot(kbar_ref[:, lo:lo + HEAD_DIM].astype(_BF), qh))

    zg = _dot(xb, wg_ref[...])

    for h in range(N_HEADS):
        gate = gates[h]
        blk = lax.broadcasted_iota(jnp.int32, gate.shape, 0)
        gate = jnp.where(blk < i, gate, -jnp.inf)
        bias = jnp.full(gate.shape, NEG_BIG, _F32)
        for _ in range(MOBA_TOPK):
            top = jnp.max(gate, axis=0, keepdims=True)
            first = jnp.min(jnp.where(gate == top, blk, n_blocks), axis=0, keepdims=True)
            hit = blk == first
            bias = jnp.where(hit, 0.0, bias)
            gate = jnp.where(hit, -jnp.inf, gate)
        bias_ref[h] = jnp.where(blk < i, bias, NEG_BIG).astype(_BF)

    kbar_rows = []
    for h in range(N_HEADS):
        lo = h * HEAD_DIM
        kh = _rms(zk[:, lo:lo + HEAD_DIM]) * kn_ref[...]
        k_ref[:, lo:lo + HEAD_DIM] = kh.astype(_BF)
        kbar_rows.append(jnp.mean(kh, axis=0, keepdims=True))
    kbar_all = kbar_ref[...]
    blk_row = lax.broadcasted_iota(jnp.int32, kbar_all.shape, 0)
    kbar_ref[...] = jnp.where(blk_row == i, jnp.concatenate(kbar_rows, axis=1), kbar_all)

    bg = zc[:, :D_MODEL]
    c = zc[:, D_MODEL:2 * D_MODEL] * zc[:, 2 * D_MODEL:]
    y = _causal_conv3(c, halo_ref[...], cw_ref[...])
    halo_ref[...] = c[PROJ_ROWS - CONV_HALO_ROWS:, :]
    branch_a = _dot((bg * y).astype(_BF), woa_ref[...])

    vt = _dot_nt(wvt_ref[...], xb)

    sa_ref[...] = jax.nn.sigmoid(zg[:, :D_MODEL]) * branch_a
    sgb_ref[...] = jax.nn.sigmoid(zg[:, D_MODEL:])
    for h in range(N_HEADS):
        vt_ref[h, 0] = vt[h * HEAD_DIM:(h + 1) * HEAD_DIM, :].astype(_BF)


def _proj(x, nm, wc, wqt, wk, wvt, wg, cw, qn, kn, woa):
    s = x.shape[0]
    assert PROJ_ROWS == MOBA_BLOCK
    nb = s // PROJ_ROWS
    const = lambda shape: pl.BlockSpec(shape, lambda i: (0,) * len(shape), pipeline_mode=pl.Buffered(1))
    rows = lambda w: pl.BlockSpec((PROJ_ROWS, w), lambda i: (i, 0))
    return pl.pallas_call(
        _proj_kernel,
        grid=(nb,),
        in_specs=[
            rows(D_MODEL),
            const((1, D_MODEL)),
            const((D_MODEL, 3 * D_MODEL)),
            const((D_MODEL, D_MODEL)),
            const((D_MODEL, D_MODEL)),
            const((D_MODEL, D_MODEL)),
            const((D_MODEL, 2 * D_MODEL)),
            const((3, D_MODEL)),
            const((HEAD_DIM, 1)),
            const((1, HEAD_DIM)),
            const((D_MODEL, D_MODEL)),
        ],
        out_specs=[
            rows(D_MODEL),
            rows(D_MODEL),
            pl.BlockSpec((D_MODEL, PROJ_ROWS), lambda i: (0, i)),
            rows(D_MODEL),
            pl.BlockSpec((N_HEADS, 1, HEAD_DIM, MOBA_BLOCK), lambda i: (0, i, 0, 0)),
            pl.BlockSpec((N_HEADS, nb, PROJ_ROWS), lambda i: (0, 0, i)),
        ],
        out_shape=[
            jax.ShapeDtypeStruct((s, D_MODEL), _F32),
            jax.ShapeDtypeStruct((s, D_MODEL), _F32),
            jax.ShapeDtypeStruct((D_MODEL, s), _BF),
            jax.ShapeDtypeStruct((s, D_MODEL), _BF),
            jax.ShapeDtypeStruct((N_HEADS, nb, HEAD_DIM, MOBA_BLOCK), _BF),
            jax.ShapeDtypeStruct((N_HEADS, nb, s), _BF),
        ],
        scratch_shapes=[
            pltpu.VMEM((CONV_HALO_ROWS, D_MODEL), _F32),
            pltpu.VMEM((nb, D_MODEL), _F32),
        ],
        compiler_params=pltpu.CompilerParams(
            dimension_semantics=("arbitrary",), vmem_limit_bytes=VMEM_LIMIT),
        name="proj",
    )(x, nm, wc, wqt, wk, wvt, wg, cw, qn, kn, woa)


def _attn_kernel(qt_ref, kblk_ref, vtblk_ref, bias_ref, o_ref, k_ref, vt_ref, qaug_ref, sa_ref, sb_ref,
                 mxa_ref, mxb_ref, m_ref, l_ref, acc_ref, *, n_blocks):
    i = pl.program_id(1)
    k_ref[i] = kblk_ref[0]
    for hh in range(ATTN_HEADS):
        vt_ref[hh, i] = vtblk_ref[hh, 0]
    ones_rows = jnp.ones((ONES_ROWS, MOBA_BLOCK), _BF)
    pad_rows = jnp.zeros((QK_DEPTH - HEAD_DIM - n_blocks, MOBA_BLOCK), _BF)
    lane = lax.broadcasted_iota(jnp.int32, (ONES_ROWS, HEAD_DIM), 1)

    def head_rows(hh):
        return slice(hh * HEAD_DIM, (hh + 1) * HEAD_DIM)

    def unit_blocks(u):
        if isinstance(u, int) and u == 0:
            return [i] + list(range(UNIT_BLOCKS - 1))
        return [jnp.minimum(UNIT_BLOCKS * u - 1 + b, i) for b in range(UNIT_BLOCKS)]

    def keys_with_onehot(j, hh, bias_row):
        row = (lane == bias_row).astype(_BF)
        onehot = jnp.concatenate([row] * (MOBA_BLOCK // ONES_ROWS), axis=0)
        return jnp.concatenate([k_ref[j, :, head_rows(hh)], onehot], axis=1)

    def put_scores(hh, s, dst_ref, mx_ref):
        dst_ref[hh] = s
        mx_ref[hh] = jnp.max(s, axis=0, keepdims=True)

    def scores(u, hh, dst_ref, mx_ref):
        ka = jnp.concatenate([keys_with_onehot(j, hh, j) for j in unit_blocks(u)], axis=0)
        put_scores(hh, _dot(ka, qaug_ref[hh]), dst_ref, mx_ref)

    def first_scores(hh, dst_ref, mx_ref):
        js = unit_blocks(0)
        ka = jnp.concatenate([keys_with_onehot(js[0], hh, n_blocks)]
                             + [keys_with_onehot(j, hh, j) for j in js[1:]], axis=0)
        s = _dot(ka, qaug_ref[hh])
        own = s[:MOBA_BLOCK]
        kpos = lax.broadcasted_iota(jnp.int32, own.shape, 0)
        qpos = lax.broadcasted_iota(jnp.int32, own.shape, 1)
        own = jnp.where(kpos <= qpos, own, NEG_BIG)
        put_scores(hh, jnp.concatenate([own, s[MOBA_BLOCK:]], axis=0), dst_ref, mx_ref)

    def update(u, hh, src_ref, mx_ref):
        m_old = m_ref[hh]
        m_new = jnp.maximum(m_old, mx_ref[hh])
        alpha = jnp.exp2(m_old - m_new)
        outs = []
        for b, j in enumerate(unit_blocks(u)):
            p = jnp.exp2(src_ref[hh, b * MOBA_BLOCK:(b + 1) * MOBA_BLOCK, :] - m_new).astype(_BF)
            outs.append(_dot(jnp.concatenate([vt_ref[hh, j], ones_rows], axis=0), p))
        out = functools.reduce(jnp.add, outs)
        m_ref[hh] = m_new
        l_ref[hh] = alpha * l_ref[hh] + out[HEAD_DIM:HEAD_DIM + 1]
        acc_ref[hh] = alpha * acc_ref[hh] + out[:HEAD_DIM]

    for hh in range(ATTN_HEADS):
        qaug_ref[hh] = jnp.concatenate([qt_ref[head_rows(hh), :], bias_ref[hh], pad_rows], axis=0)
        m_ref[hh] = jnp.full(m_ref.shape[1:], NEG_BIG, _F32)
        l_ref[hh] = jnp.zeros(l_ref.shape[1:], _F32)
        acc_ref[hh] = jnp.zeros(acc_ref.shape[1:], _F32)
    for hh in range(ATTN_HEADS):
        first_scores(hh, sa_ref, mxa_ref)

    def unit_pairs(u0, n_pairs):
        for pair in range(n_pairs):
            u = u0 + 2 * pair
            for hh in range(ATTN_HEADS):
                scores(u + 1, hh, sb_ref, mxb_ref)
                update(u, hh, sa_ref, mxa_ref)
            for hh in range(ATTN_HEADS):
                scores(u + 2, hh, sa_ref, mxa_ref)
                update(u + 1, hh, sb_ref, mxb_ref)

    unit_pairs(0, 1)
    n_units = lax.shift_right_logical(i + UNIT_BLOCKS, UNIT_SHIFT)
    more_pairs = lax.shift_right_logical(n_units + 1, 1) - 1
    long_trips = lax.shift_right_logical(more_pairs, LONG_TRIP_SHIFT)

    def long_body(t, carry):
        unit_pairs(2 + 2 * LONG_TRIP_PAIRS * t, LONG_TRIP_PAIRS)
        return carry

    def short_body(t, carry):
        unit_pairs(2 + 2 * (LONG_TRIP_PAIRS * long_trips + t), 1)
        return carry

    lax.fori_loop(0, long_trips, long_body, 0)
    lax.fori_loop(0, more_pairs - LONG_TRIP_PAIRS * long_trips, short_body, 0)
    for hh in range(ATTN_HEADS):
        o_ref[:, head_rows(hh)] = (acc_ref[hh] * (1.0 / l_ref[hh])).T.astype(o_ref.dtype)


def _attn(qt, k, vt, bias):
    s = k.shape[0]
    nb = s // MOBA_BLOCK
    assert nb < QK_DEPTH - HEAD_DIM and nb % (2 * UNIT_BLOCKS) == 0
    k3 = k.reshape(nb, MOBA_BLOCK, D_MODEL)
    gw = ATTN_HEADS * HEAD_DIM
    scores_buf = pltpu.VMEM((ATTN_HEADS, UNIT_KEYS, MOBA_BLOCK), _F32)
    row_buf = pltpu.VMEM((ATTN_HEADS, 1, MOBA_BLOCK), _F32)
    return pl.pallas_call(
        functools.partial(_attn_kernel, n_blocks=nb),
        grid=(N_HEADS // ATTN_HEADS, nb),
        in_specs=[
            pl.BlockSpec((gw, MOBA_BLOCK), lambda g, i: (g, i)),
            pl.BlockSpec((1, MOBA_BLOCK, gw), lambda g, i: (i, 0, g)),
            pl.BlockSpec((ATTN_HEADS, 1, HEAD_DIM, MOBA_BLOCK), lambda g, i: (g, i, 0, 0)),
            pl.BlockSpec((ATTN_HEADS, nb, MOBA_BLOCK), lambda g, i: (g, 0, i)),
        ],
        out_specs=pl.BlockSpec((MOBA_BLOCK, gw), lambda g, i: (i, g)),
        out_shape=jax.ShapeDtypeStruct((s, D_MODEL), _BF),
        scratch_shapes=[
            pltpu.VMEM((nb, MOBA_BLOCK, gw), _BF),
            pltpu.VMEM((ATTN_HEADS, nb, HEAD_DIM, MOBA_BLOCK), _BF),
            pltpu.VMEM((ATTN_HEADS, QK_DEPTH, MOBA_BLOCK), _BF),
            scores_buf,
            scores_buf,
            row_buf,
            row_buf,
            row_buf,
            row_buf,
            pltpu.VMEM((ATTN_HEADS, HEAD_DIM, MOBA_BLOCK), _F32),
        ],
        compiler_params=pltpu.CompilerParams(
            dimension_semantics=("arbitrary", "arbitrary"), vmem_limit_bytes=VMEM_LIMIT),
        name="moba_attn",
    )(qt, k3, vt, bias)


def _post_kernel(o_ref, sa_ref, sgb_ref, x_ref, wob_ref, wo_ref, nf_ref, wup_ref, fcw_ref, fcb_ref,
                 wdn_ref, out_ref, halo_ref, ubuf_ref):
    i = pl.program_id(0)

    @pl.when(i == 0)
    def _():
        halo_ref[...] = jnp.zeros_like(halo_ref)

    branch_b = _dot(o_ref[...], wob_ref[...])
    merged = sa_ref[...] + sgb_ref[...] * branch_b
    h = x_ref[...] + _dot(merged.astype(_BF), wo_ref[...])
    hn = (_rms(h) * nf_ref[...]).astype(_BF)

    n_chunks = D_FF // FFN_CHUNK

    def up_proj(c):
        for part in range(2):
            lo = part * D_FF + c * FFN_CHUNK
            u = _dot(hn, wup_ref[:, lo:lo + FFN_CHUNK])
            buf = ubuf_ref.at[c % (FFN_LOOKAHEAD + 1), part]
            buf[:CONV_HALO_ROWS, :] = halo_ref[:, lo:lo + FFN_CHUNK]
            buf[CONV_HALO_ROWS:, :] = u
            halo_ref[:, lo:lo + FFN_CHUNK] = u[POST_ROWS - CONV_HALO_ROWS:, :]

    def conv(c, part):
        lo = part * D_FF + c * FFN_CHUNK
        buf = ubuf_ref.at[c % (FFN_LOOKAHEAD + 1), part]
        w = fcw_ref[:, lo:lo + FFN_CHUNK]
        taps = [buf[CONV_HALO_ROWS - 2 + j:CONV_HALO_ROWS - 2 + j + POST_ROWS, :] for j in range(3)]
        return (w[0:1, :] * taps[0] + w[1:2, :] * taps[1] + w[2:3, :] * taps[2]
                + fcb_ref[:, lo:lo + FFN_CHUNK])

    acc = h
    for c in range(min(FFN_LOOKAHEAD, n_chunks)):
        up_proj(c)
    for c in range(n_chunks):
        if c + FFN_LOOKAHEAD < n_chunks:
            up_proj(c + FFN_LOOKAHEAD)
        act = (jax.nn.silu(conv(c, 0)) * conv(c, 1)).astype(_BF)
        acc = acc + _dot(act, wdn_ref[c * FFN_CHUNK:(c + 1) * FFN_CHUNK, :])
    out_ref[...] = acc


def _post(o, sa, sgb, x, wob, wo, nf, wup, fcw, fcb, wdn):
    s = x.shape[0]
    const = lambda shape: pl.BlockSpec(shape, lambda i: (0,) * len(shape), pipeline_mode=pl.Buffered(1))
    rows = pl.BlockSpec((POST_ROWS, D_MODEL), lambda i: (i, 0))
    return pl.pallas_call(
        _post_kernel,
        grid=(s // POST_ROWS,),
        in_specs=[
            rows, rows, rows, rows,
            const((D_MODEL, D_MODEL)),
            const((D_MODEL, D_MODEL)),
            const((1, D_MODEL)),
            const((D_MODEL, 2 * D_FF)),
            const((3, 2 * D_FF)),
            const((1, 2 * D_FF)),
            const((D_FF, D_MODEL)),
        ],
        out_specs=rows,
        out_shape=jax.ShapeDtypeStruct((s, D_MODEL), _F32),
        scratch_shapes=[
            pltpu.VMEM((CONV_HALO_ROWS, 2 * D_FF), _F32),
            pltpu.VMEM((FFN_LOOKAHEAD + 1, 2, CONV_HALO_ROWS + POST_ROWS, FFN_CHUNK), _F32),
        ],
        compiler_params=pltpu.CompilerParams(
            dimension_semantics=("arbitrary",), vmem_limit_bytes=VMEM_LIMIT),
        name="post",
    )(o, sa, sgb, x, wob, wo, nf, wup, fcw, fcb, wdn)


def kernel(x, norm_mix, w_in, conv_a_w, q_norm, k_norm, w_out_a, w_out_b, w_o, norm_ffn, w_up,
           ffn_conv_w, ffn_conv_b, w_down):
    b, s, d = x.shape
    assert b == 1 and d == D_MODEL and s % POST_ROWS == 0 and s % MOBA_BLOCK == 0
    x2 = x.reshape(s, d)
    wc = w_in[:, :3 * D_MODEL].astype(_BF)
    wqt = w_in[:, 3 * D_MODEL:4 * D_MODEL].T.astype(_BF)
    wk = w_in[:, 4 * D_MODEL:5 * D_MODEL].astype(_BF)
    wvt = w_in[:, 5 * D_MODEL:6 * D_MODEL].T.astype(_BF)
    wg = w_in[:, 6 * D_MODEL:].astype(_BF)

    sa, sgb, qt, k, vt, bias = _proj(
        x2, norm_mix.reshape(1, d), wc, wqt, wk, wvt, wg, conv_a_w, q_norm.reshape(HEAD_DIM, 1),
        k_norm.reshape(1, HEAD_DIM), w_out_a.astype(_BF))
    o = _attn(qt, k, vt, bias)
    out = _post(o, sa, sgb, x2, w_out_b.astype(_BF), w_o.astype(_BF), norm_ffn.reshape(1, d),
                w_up.astype(_BF), ffn_conv_w, ffn_conv_b.reshape(1, 2 * D_FF), w_down.astype(_BF))
    return out.reshape(b, s, d)
```

```python
import functools

import jax
import jax.numpy as jnp
from jax import lax
from jax.experimental import pallas as pl
from jax.experimental.pallas import tpu as pltpu

D_MODEL = 1024
N_HEADS = 8
HEAD_DIM = 128
MOBA_BLOCK = 256
MOBA_TOPK = 3
D_FF = 2816
EPS = 1e-6
CONV_HALO_ROWS = 8
NEG_BIG = -1e30

PROJ_ROWS = 256
POST_ROWS = 512
FFN_CHUNK = 256
FFN_LOOKAHEAD = 3
VMEM_LIMIT = 56 * 1024 * 1024
ATTN_HEADS = 4
ONES_ROWS = 16
UNIT_SHIFT = 1
UNIT_BLOCKS = 1 << UNIT_SHIFT
UNIT_KEYS = UNIT_BLOCKS * MOBA_BLOCK
Q_BLOCKS = 2
Q_TILE = Q_BLOCKS * MOBA_BLOCK
LONG_TRIP_SHIFT = 1
LONG_TRIP_PAIRS = 1 << LONG_TRIP_SHIFT
QK_DEPTH = 2 * HEAD_DIM
QK_SCALE = HEAD_DIM ** -0.5 * 1.4426950408889634

_BF = jnp.bfloat16
_F32 = jnp.float32


def _dot(a, b):
    return jnp.dot(a, b, preferred_element_type=_F32)


def _dot_nt(a, b):
    return lax.dot_general(a, b, (((1,), (1,)), ((), ())), preferred_element_type=_F32)


def _rms(x, axis=-1):
    return x * lax.rsqrt(jnp.mean(x * x, axis=axis, keepdims=True) + EPS)


def _causal_conv3(c, prev, w):
    row = lax.broadcasted_iota(jnp.int32, c.shape, 0)
    p1 = prev[CONV_HALO_ROWS - 1:CONV_HALO_ROWS, :]
    p2 = prev[CONV_HALO_ROWS - 2:CONV_HALO_ROWS - 1, :]
    c1 = jnp.where(row == 0, p1, pltpu.roll(c, 1, 0))
    c2 = jnp.where(row == 0, p2, jnp.where(row == 1, p1, pltpu.roll(c, 2, 0)))
    return w[0:1, :] * c2 + w[1:2, :] * c1 + w[2:3, :] * c


def _proj_kernel(x_ref, nm_ref, wc_ref, wqt_ref, wk_ref, wvt_ref, wg_ref, cw_ref, qn_ref, kn_ref, woa_ref,
                 sa_ref, sgb_ref, qt_ref, k_ref, vt_ref, bias_ref, halo_ref, kbar_ref):
    i = pl.program_id(0)

    @pl.when(i == 0)
    def _():
        halo_ref[...] = jnp.zeros_like(halo_ref)
        kbar_ref[...] = jnp.zeros_like(kbar_ref)

    xb = (_rms(x_ref[...]) * nm_ref[...]).astype(_BF)
    n_blocks = kbar_ref.shape[0]

    zk = _dot(xb, wk_ref[...])
    zqt = _dot_nt(wqt_ref[...], xb)
    zc = _dot(xb, wc_ref[...])

    gates = []
    for h in range(N_HEADS):
        lo = h * HEAD_DIM
        qh = (_rms(zqt[lo:lo + HEAD_DIM, :], axis=0) * qn_ref[...] * QK_SCALE).astype(_BF)
        qt_ref[lo:lo + HEAD_DIM, :] = qh
        gates.append(_dot(kbar_ref[:, lo:lo + HEAD_DIM].astype(_BF), qh))

    zg = _dot(xb, wg_ref[...])

    for h in range(N_HEADS):
        gate = gates[h]
        blk = lax.broadcasted_iota(jnp.int32, gate.shape, 0)
        gate = jnp.where(blk < i, gate, -jnp.inf)
        bias = jnp.full(gate.shape, NEG_BIG, _F32)
        for _ in range(MOBA_TOPK):
            top = jnp.max(gate, axis=0, keepdims=True)
            first = jnp.min(jnp.where(gate == top, blk, n_blocks), axis=0, keepdims=True)
            hit = blk == first
            bias = jnp.where(hit, 0.0, bias)
            gate = jnp.where(hit, -jnp.inf, gate)
        bias_ref[h] = jnp.where(blk < i, bias, NEG_BIG).astype(_BF)

    kbar_rows = []
    for h in range(N_HEADS):
        lo = h * HEAD_DIM
        kh = _rms(zk[:, lo:lo + HEAD_DIM]) * kn_ref[...]
        k_ref[:, lo:lo + HEAD_DIM] = kh.astype(_BF)
        kbar_rows.append(jnp.mean(kh, axis=0, keepdims=True))
    kbar_all = kbar_ref[...]
    blk_row = lax.broadcasted_iota(jnp.int32, kbar_all.shape, 0)
    kbar_ref[...] = jnp.where(blk_row == i, jnp.concatenate(kbar_rows, axis=1), kbar_all)

    bg = zc[:, :D_MODEL]
    c = zc[:, D_MODEL:2 * D_MODEL] * zc[:, 2 * D_MODEL:]
    y = _causal_conv3(c, halo_ref[...], cw_ref[...])
    halo_ref[...] = c[PROJ_ROWS - CONV_HALO_ROWS:, :]
    branch_a = _dot((bg * y).astype(_BF), woa_ref[...])

    vt = _dot_nt(wvt_ref[...], xb)

    sa_ref[...] = jax.nn.sigmoid(zg[:, :D_MODEL]) * branch_a
    sgb_ref[...] = jax.nn.sigmoid(zg[:, D_MODEL:])
    for h in range(N_HEADS):
        vt_ref[h, 0] = vt[h * HEAD_DIM:(h + 1) * HEAD_DIM, :].astype(_BF)


def _proj(x, nm, wc, wqt, wk, wvt, wg, cw, qn, kn, woa):
    s = x.shape[0]
    assert PROJ_ROWS == MOBA_BLOCK
    nb = s // PROJ_ROWS
    const = lambda shape: pl.BlockSpec(shape, lambda i: (0,) * len(shape), pipeline_mode=pl.Buffered(1))
    rows = lambda w: pl.BlockSpec((PROJ_ROWS, w), lambda i: (i, 0))
    return pl.pallas_call(
        _proj_kernel,
        grid=(nb,),
        in_specs=[
            rows(D_MODEL),
            const((1, D_MODEL)),
            const((D_MODEL, 3 * D_MODEL)),
            const((D_MODEL, D_MODEL)),
            const((D_MODEL, D_MODEL)),
            const((D_MODEL, D_MODEL)),
            const((D_MODEL, 2 * D_MODEL)),
            const((3, D_MODEL)),
            const((HEAD_DIM, 1)),
            const((1, HEAD_DIM)),
            const((D_MODEL, D_MODEL)),
        ],
        out_specs=[
            rows(D_MODEL),
            rows(D_MODEL),
            pl.BlockSpec((D_MODEL, PROJ_ROWS), lambda i: (0, i)),
            rows(D_MODEL),
            pl.BlockSpec((N_HEADS, 1, HEAD_DIM, MOBA_BLOCK), lambda i: (0, i, 0, 0)),
            pl.BlockSpec((N_HEADS, nb, PROJ_ROWS), lambda i: (0, 0, i)),
        ],
        out_shape=[
            jax.ShapeDtypeStruct((s, D_MODEL), _F32),
            jax.ShapeDtypeStruct((s, D_MODEL), _F32),
            jax.ShapeDtypeStruct((D_MODEL, s), _BF),
            jax.ShapeDtypeStruct((s, D_MODEL), _BF),
            jax.ShapeDtypeStruct((N_HEADS, nb, HEAD_DIM, MOBA_BLOCK), _BF),
            jax.ShapeDtypeStruct((N_HEADS, nb, s), _BF),
        ],
        scratch_shapes=[
            pltpu.VMEM((CONV_HALO_ROWS, D_MODEL), _F32),
            pltpu.VMEM((nb, D_MODEL), _F32),
        ],
        compiler_params=pltpu.CompilerParams(
            dimension_semantics=("arbitrary",), vmem_limit_bytes=VMEM_LIMIT),
        name="proj",
    )(x, nm, wc, wqt, wk, wvt, wg, cw, qn, kn, woa)


def _attn_kernel(qt_ref, kblk_ref, vtblk_ref, bias_ref, o_ref, k_ref, vt_ref, qaug_ref, sa_ref, sb_ref,
                 mxa_ref, mxb_ref, m_ref, l_ref, acc_ref, *, n_blocks):
    a = pl.program_id(1)
    own0 = Q_BLOCKS * a
    own1 = own0 + 1
    for b in range(Q_BLOCKS):
        k_ref[own0 + b] = kblk_ref[b]
        for hh in range(ATTN_HEADS):
            vt_ref[hh, own0 + b] = vtblk_ref[hh, b]
    ones_rows = jnp.ones((ONES_ROWS, MOBA_BLOCK), _BF)
    pad_rows = jnp.zeros((QK_DEPTH - HEAD_DIM - n_blocks, Q_TILE), _BF)
    lane = lax.broadcasted_iota(jnp.int32, (ONES_ROWS, HEAD_DIM), 1)

    def head_rows(hh):
        return slice(hh * HEAD_DIM, (hh + 1) * HEAD_DIM)

    def unit_blocks(u):
        js = [UNIT_BLOCKS * (u - 1) + b for b in range(UNIT_BLOCKS)]
        return [jnp.where(j < own0, j, own1) for j in js]

    def keys_with_onehot(j, hh, bias_row):
        row = (lane == bias_row).astype(_BF)
        onehot = jnp.concatenate([row] * (MOBA_BLOCK // ONES_ROWS), axis=0)
        return jnp.concatenate([k_ref[j, :, head_rows(hh)], onehot], axis=1)

    def put_scores(hh, s, dst_ref, mx_ref):
        dst_ref[hh] = s
        mx_ref[hh] = jnp.max(s, axis=0, keepdims=True)

    def scores(u, hh, dst_ref, mx_ref):
        ka = jnp.concatenate([keys_with_onehot(j, hh, j) for j in unit_blocks(u)], axis=0)
        put_scores(hh, _dot(ka, qaug_ref[hh]), dst_ref, mx_ref)

    def first_scores(hh, dst_ref, mx_ref):
        ka = jnp.concatenate([keys_with_onehot(own0, hh, n_blocks), keys_with_onehot(own1, hh, n_blocks)], axis=0)
        s = _dot(ka, qaug_ref[hh])
        kpos = lax.broadcasted_iota(jnp.int32, (MOBA_BLOCK, Q_TILE), 0)
        qcol = lax.broadcasted_iota(jnp.int32, (1, Q_TILE), 1)
        bias = bias_ref[hh].astype(_F32)
        blk = lax.broadcasted_iota(jnp.int32, bias.shape, 0)
        picked = jnp.max(jnp.where(blk == own0, bias, NEG_BIG), axis=0, keepdims=True) >= 0.0
        last0 = jnp.where(qcol < MOBA_BLOCK, qcol, jnp.where(picked, MOBA_BLOCK, -1))
        last1 = qcol - MOBA_BLOCK
        s = jnp.concatenate([jnp.where(kpos <= last0, s[:MOBA_BLOCK], NEG_BIG),
                             jnp.where(kpos <= last1, s[MOBA_BLOCK:], NEG_BIG)], axis=0)
        put_scores(hh, s, dst_ref, mx_ref)

    def update(blocks, hh, src_ref, mx_ref):
        m_old = m_ref[hh]
        m_new = jnp.maximum(m_old, mx_ref[hh])
        alpha = jnp.exp2(m_old - m_new)
        outs = []
        for b, j in enumerate(blocks):
            p = jnp.exp2(src_ref[hh, b * MOBA_BLOCK:(b + 1) * MOBA_BLOCK, :] - m_new).astype(_BF)
            outs.append(_dot(jnp.concatenate([vt_ref[hh, j], ones_rows], axis=0), p))
        out = functools.reduce(jnp.add, outs)
        m_ref[hh] = m_new
        l_ref[hh] = alpha * l_ref[hh] + out[HEAD_DIM:HEAD_DIM + 1]
        acc_ref[hh] = alpha * acc_ref[hh] + out[:HEAD_DIM]

    for hh in range(ATTN_HEADS):
        qaug_ref[hh] = jnp.concatenate([qt_ref[head_rows(hh), :], bias_ref[hh], pad_rows], axis=0)
        m_ref[hh] = jnp.full(m_ref.shape[1:], NEG_BIG, _F32)
        l_ref[hh] = jnp.zeros(l_ref.shape[1:], _F32)
        acc_ref[hh] = jnp.zeros(acc_ref.shape[1:], _F32)
    for hh in range(ATTN_HEADS):
        first_scores(hh, sa_ref, mxa_ref)

    def unit_pairs(u0, n_pairs, first=False):
        for pair in range(n_pairs):
            u = u0 + 2 * pair
            for hh in range(ATTN_HEADS):
                scores(u + 1, hh, sb_ref, mxb_ref)
                update([own0, own1] if first and pair == 0 else unit_blocks(u), hh, sa_ref, mxa_ref)
            for hh in range(ATTN_HEADS):
                scores(u + 2, hh, sa_ref, mxa_ref)
                update(unit_blocks(u + 1), hh, sb_ref, mxb_ref)

    unit_pairs(0, 1, first=True)
    n_units = a + 1
    more_pairs = lax.shift_right_logical(n_units + 1, 1) - 1
    long_trips = lax.shift_right_logical(more_pairs, LONG_TRIP_SHIFT)

    def long_body(t, carry):
        unit_pairs(2 + 2 * LONG_TRIP_PAIRS * t, LONG_TRIP_PAIRS)
        return carry

    def short_body(t, carry):
        unit_pairs(2 + 2 * (LONG_TRIP_PAIRS * long_trips + t), 1)
        return carry

    lax.fori_loop(0, long_trips, long_body, 0)
    lax.fori_loop(0, more_pairs - LONG_TRIP_PAIRS * long_trips, short_body, 0)
    for hh in range(ATTN_HEADS):
        o_ref[:, head_rows(hh)] = (acc_ref[hh] * (1.0 / l_ref[hh])).T.astype(o_ref.dtype)


def _attn(qt, k, vt, bias):
    s = k.shape[0]
    nb = s // MOBA_BLOCK
    assert nb < QK_DEPTH - HEAD_DIM and nb % Q_BLOCKS == 0 and UNIT_BLOCKS == Q_BLOCKS == 2
    k3 = k.reshape(nb, MOBA_BLOCK, D_MODEL)
    gw = ATTN_HEADS * HEAD_DIM
    scores_buf = pltpu.VMEM((ATTN_HEADS, UNIT_KEYS, Q_TILE), _F32)
    row_buf = pltpu.VMEM((ATTN_HEADS, 1, Q_TILE), _F32)
    return pl.pallas_call(
        functools.partial(_attn_kernel, n_blocks=nb),
        grid=(N_HEADS // ATTN_HEADS, nb // Q_BLOCKS),
        in_specs=[
            pl.BlockSpec((gw, Q_TILE), lambda g, a: (g, a)),
            pl.BlockSpec((Q_BLOCKS, MOBA_BLOCK, gw), lambda g, a: (a, 0, g)),
            pl.BlockSpec((ATTN_HEADS, Q_BLOCKS, HEAD_DIM, MOBA_BLOCK), lambda g, a: (g, a, 0, 0)),
            pl.BlockSpec((ATTN_HEADS, nb, Q_TILE), lambda g, a: (g, 0, a)),
        ],
        out_specs=pl.BlockSpec((Q_TILE, gw), lambda g, a: (a, g)),
        out_shape=jax.ShapeDtypeStruct((s, D_MODEL), _BF),
        scratch_shapes=[
            pltpu.VMEM((nb, MOBA_BLOCK, gw), _BF),
            pltpu.VMEM((ATTN_HEADS, nb, HEAD_DIM, MOBA_BLOCK), _BF),
            pltpu.VMEM((ATTN_HEADS, QK_DEPTH, Q_TILE), _BF),
            scores_buf,
            scores_buf,
            row_buf,
            row_buf,
            row_buf,
            row_buf,
            pltpu.VMEM((ATTN_HEADS, HEAD_DIM, Q_TILE), _F32),
        ],
        compiler_params=pltpu.CompilerParams(
            dimension_semantics=("arbitrary", "arbitrary"), vmem_limit_bytes=VMEM_LIMIT),
        name="moba_attn",
    )(qt, k3, vt, bias)


def _post_kernel(o_ref, sa_ref, sgb_ref, x_ref, wob_ref, wo_ref, nf_ref, wup_ref, fcw_ref, fcb_ref,
                 wdn_ref, out_ref, halo_ref, ubuf_ref):
    i = pl.program_id(0)

    @pl.when(i == 0)
    def _():
        halo_ref[...] = jnp.zeros_like(halo_ref)

    branch_b = _dot(o_ref[...], wob_ref[...])
    merged = sa_ref[...] + sgb_ref[...] * branch_b
    h = x_ref[...] + _dot(merged.astype(_BF), wo_ref[...])
    hn = (_rms(h) * nf_ref[...]).astype(_BF)

    n_chunks = D_FF // FFN_CHUNK

    def up_proj(c):
        for part in range(2):
            lo = part * D_FF + c * FFN_CHUNK
            u = _dot(hn, wup_ref[:, lo:lo + FFN_CHUNK])
            buf = ubuf_ref.at[c % (FFN_LOOKAHEAD + 1), part]
            buf[:CONV_HALO_ROWS, :] = halo_ref[:, lo:lo + FFN_CHUNK]
            buf[CONV_HALO_ROWS:, :] = u
            halo_ref[:, lo:lo + FFN_CHUNK] = u[POST_ROWS - CONV_HALO_ROWS:, :]

    def conv(c, part):
        lo = part * D_FF + c * FFN_CHUNK
        buf = ubuf_ref.at[c % (FFN_LOOKAHEAD + 1), part]
        w = fcw_ref[:, lo:lo + FFN_CHUNK]
        taps = [buf[CONV_HALO_ROWS - 2 + j:CONV_HALO_ROWS - 2 + j + POST_ROWS, :] for j in range(3)]
        return (w[0:1, :] * taps[0] + w[1:2, :] * taps[1] + w[2:3, :] * taps[2]
                + fcb_ref[:, lo:lo + FFN_CHUNK])

    acc = h
    for c in range(min(FFN_LOOKAHEAD, n_chunks)):
        up_proj(c)
    for c in range(n_chunks):
        if c + FFN_LOOKAHEAD < n_chunks:
            up_proj(c + FFN_LOOKAHEAD)
        act = (jax.nn.silu(conv(c, 0)) * conv(c, 1)).astype(_BF)
        acc = acc + _dot(act, wdn_ref[c * FFN_CHUNK:(c + 1) * FFN_CHUNK, :])
    out_ref[...] = acc


def _post(o, sa, sgb, x, wob, wo, nf, wup, fcw, fcb, wdn):
    s = x.shape[0]
    const = lambda shape: pl.BlockSpec(shape, lambda i: (0,) * len(shape), pipeline_mode=pl.Buffered(1))
    rows = pl.BlockSpec((POST_ROWS, D_MODEL), lambda i: (i, 0))
    return pl.pallas_call(
        _post_kernel,
        grid=(s // POST_ROWS,),
        in_specs=[
            rows, rows, rows, rows,
            const((D_MODEL, D_MODEL)),
            const((D_MODEL, D_MODEL)),
            const((1, D_MODEL)),
            const((D_MODEL, 2 * D_FF)),
            const((3, 2 * D_FF)),
            const((1, 2 * D_FF)),
            const((D_FF, D_MODEL)),
        ],
        out_specs=rows,
        out_shape=jax.ShapeDtypeStruct((s, D_MODEL), _F32),
        scratch_shapes=[
            pltpu.VMEM((CONV_HALO_ROWS, 2 * D_FF), _F32),
            pltpu.VMEM((FFN_LOOKAHEAD + 1, 2, CONV_HALO_ROWS + POST_ROWS, FFN_CHUNK), _F32),
        ],
        compiler_params=pltpu.CompilerParams(
            dimension_semantics=("arbitrary",), vmem_limit_bytes=VMEM_LIMIT),
        name="post",
    )(o, sa, sgb, x, wob, wo, nf, wup, fcw, fcb, wdn)


def kernel(x, norm_mix, w_in, conv_a_w, q_norm, k_norm, w_out_a, w_out_b, w_o, norm_ffn, w_up,
           ffn_conv_w, ffn_conv_b, w_down):
    b, s, d = x.shape
    assert b == 1 and d == D_MODEL and s % POST_ROWS == 0 and s % MOBA_BLOCK == 0
    x2 = x.reshape(s, d)
    wc = w_in[:, :3 * D_MODEL].astype(_BF)
    wqt = w_in[:, 3 * D_MODEL:4 * D_MODEL].T.astype(_BF)
    wk = w_in[:, 4 * D_MODEL:5 * D_MODEL].astype(_BF)
    wvt = w_in[:, 5 * D_MODEL:6 * D_MODEL].T.astype(_BF)
    wg = w_in[:, 6 * D_MODEL:].astype(_BF)

    sa, sgb, qt, k, vt, bias = _proj(
        x2, norm_mix.reshape(1, d), wc, wqt, wk, wvt, wg, conv_a_w, q_norm.reshape(HEAD_DIM, 1),
        k_norm.reshape(1, HEAD_DIM), w_out_a.astype(_BF))
    o = _attn(qt, k, vt, bias)
    out = _post(o, sa, sgb, x2, w_out_b.astype(_BF), w_o.astype(_BF), norm_ffn.reshape(1, d),
                w_up.astype(_BF), ffn_conv_w, ffn_conv_b.reshape(1, 2 * D_FF), w_down.astype(_BF))
    return out.reshape(b, s, d)
```

```python
import functools

import jax
import jax.numpy as jnp
from jax import lax
from jax.experimental import pallas as pl
from jax.experimental.pallas import tpu as pltpu

D_MODEL = 1024
N_HEADS = 8
HEAD_DIM = 128
MOBA_BLOCK = 256
MOBA_TOPK = 3
D_FF = 2816
EPS = 1e-6
CONV_HALO_ROWS = 8
NEG_BIG = -1e30

PROJ_ROWS = 256
WDN_CAST_STEPS = 16
POST_ROWS = 512
FFN_CHUNK = 256
FFN_LOOKAHEAD = 3
VMEM_LIMIT = 56 * 1024 * 1024
ATTN_HEADS = 4
ONES_ROWS = 16
UNIT_SHIFT = 1
UNIT_BLOCKS = 1 << UNIT_SHIFT
UNIT_KEYS = UNIT_BLOCKS * MOBA_BLOCK
Q_BLOCKS = 2
Q_TILE = Q_BLOCKS * MOBA_BLOCK
LONG_TRIP_SHIFT = 1
LONG_TRIP_PAIRS = 1 << LONG_TRIP_SHIFT
QK_DEPTH = 2 * HEAD_DIM
QK_SCALE = HEAD_DIM ** -0.5 * 1.4426950408889634

_BF = jnp.bfloat16
_F32 = jnp.float32


def _dot(a, b):
    return jnp.dot(a, b, preferred_element_type=_F32)


def _dot_nt(a, b):
    return lax.dot_general(a, b, (((1,), (1,)), ((), ())), preferred_element_type=_F32)


def _rms(x, axis=-1):
    return x * lax.rsqrt(jnp.mean(x * x, axis=axis, keepdims=True) + EPS)


def _causal_conv3(c, prev, w):
    row = lax.broadcasted_iota(jnp.int32, c.shape, 0)
    p1 = prev[CONV_HALO_ROWS - 1:CONV_HALO_ROWS, :]
    p2 = prev[CONV_HALO_ROWS - 2:CONV_HALO_ROWS - 1, :]
    c1 = jnp.where(row == 0, p1, pltpu.roll(c, 1, 0))
    c2 = jnp.where(row == 0, p2, jnp.where(row == 1, p1, pltpu.roll(c, 2, 0)))
    return w[0:1, :] * c2 + w[1:2, :] * c1 + w[2:3, :] * c


def _proj_kernel(x_ref, nm_ref, wc_ref, wqt_ref, wk_ref, wvt_ref, wg_ref, cw_ref, qn_ref, kn_ref, woa_ref,
                 wup32_ref, wob32_ref, wo32_ref, wdn32_ref,
                 sa_ref, sgb_ref, qt_ref, k_ref, vt_ref, bias_ref, wup_ref, wob_ref, wo_ref, wdn_ref,
                 halo_ref, kbar_ref):
    i = pl.program_id(0)

    @pl.when(i == 0)
    def _():
        halo_ref[...] = jnp.zeros_like(halo_ref)
        kbar_ref[...] = jnp.zeros_like(kbar_ref)

    wup_ref[...] = wup32_ref[...].astype(_BF)
    wob_ref[...] = wob32_ref[...].astype(_BF)
    wo_ref[...] = wo32_ref[...].astype(_BF)

    @pl.when(i < WDN_CAST_STEPS)
    def _():
        wdn_ref[...] = wdn32_ref[...].astype(_BF)

    xb = (_rms(x_ref[...]) * nm_ref[...]).astype(_BF)
    n_blocks = kbar_ref.shape[0]

    zk = _dot(xb, wk_ref[...])
    zqt = _dot_nt(wqt_ref[...], xb)
    zc = _dot(xb, wc_ref[...])

    gates = []
    for h in range(N_HEADS):
        lo = h * HEAD_DIM
        qh = (_rms(zqt[lo:lo + HEAD_DIM, :], axis=0) * qn_ref[...] * QK_SCALE).astype(_BF)
        qt_ref[lo:lo + HEAD_DIM, :] = qh
        gates.append(_dot(kbar_ref[:, lo:lo + HEAD_DIM].astype(_BF), qh))

    zg = _dot(xb, wg_ref[...])

    for h in range(N_HEADS):
        gate = gates[h]
        blk = lax.broadcasted_iota(jnp.int32, gate.shape, 0)
        gate = jnp.where(blk < i, gate, -jnp.inf)
        bias = jnp.full(gate.shape, NEG_BIG, _F32)
        for _ in range(MOBA_TOPK):
            top = jnp.max(gate, axis=0, keepdims=True)
            first = jnp.min(jnp.where(gate == top, blk, n_blocks), axis=0, keepdims=True)
            hit = blk == first
            bias = jnp.where(hit, 0.0, bias)
            gate = jnp.where(hit, -jnp.inf, gate)
        bias_ref[h] = jnp.where(blk < i, bias, NEG_BIG).astype(_BF)

    kbar_rows = []
    for h in range(N_HEADS):
        lo = h * HEAD_DIM
        kh = _rms(zk[:, lo:lo + HEAD_DIM]) * kn_ref[...]
        k_ref[:, lo:lo + HEAD_DIM] = kh.astype(_BF)
        kbar_rows.append(jnp.mean(kh, axis=0, keepdims=True))
    kbar_all = kbar_ref[...]
    blk_row = lax.broadcasted_iota(jnp.int32, kbar_all.shape, 0)
    kbar_ref[...] = jnp.where(blk_row == i, jnp.concatenate(kbar_rows, axis=1), kbar_all)

    bg = zc[:, :D_MODEL]
    c = zc[:, D_MODEL:2 * D_MODEL] * zc[:, 2 * D_MODEL:]
    y = _causal_conv3(c, halo_ref[...], cw_ref[...])
    halo_ref[...] = c[PROJ_ROWS - CONV_HALO_ROWS:, :]
    branch_a = _dot((bg * y).astype(_BF), woa_ref[...])

    vt = _dot_nt(wvt_ref[...], xb)

    sa_ref[...] = jax.nn.sigmoid(zg[:, :D_MODEL]) * branch_a
    sgb_ref[...] = jax.nn.sigmoid(zg[:, D_MODEL:])
    for h in range(N_HEADS):
        vt_ref[h, 0] = vt[h * HEAD_DIM:(h + 1) * HEAD_DIM, :].astype(_BF)


def _proj(x, nm, wb, wqt, wvt, cw, qn, kn, woa, w_up, w_out_b, w_o, w_down):
    s = x.shape[0]
    assert PROJ_ROWS == MOBA_BLOCK
    nb = s // PROJ_ROWS
    cast_rows = D_MODEL // nb
    wdn_rows = D_FF // WDN_CAST_STEPS
    assert cast_rows * nb == D_MODEL and cast_rows % 16 == 0 and wdn_rows % 16 == 0 and nb >= WDN_CAST_STEPS
    const = lambda shape: pl.BlockSpec(shape, lambda i: (0,) * len(shape), pipeline_mode=pl.Buffered(1))
    w_in_cols = lambda width, blk: pl.BlockSpec((D_MODEL, width), lambda i: (0, blk),
                                                pipeline_mode=pl.Buffered(1))
    rows = lambda w: pl.BlockSpec((PROJ_ROWS, w), lambda i: (i, 0))
    slab = lambda w: pl.BlockSpec((cast_rows, w), lambda i: (i, 0))
    wdn_slab = pl.BlockSpec((wdn_rows, D_MODEL), lambda i: (jnp.minimum(i, WDN_CAST_STEPS - 1), 0))
    return pl.pallas_call(
        _proj_kernel,
        grid=(nb,),
        in_specs=[
            rows(D_MODEL),
            const((1, D_MODEL)),
            w_in_cols(3 * D_MODEL, 0),
            const((D_MODEL, D_MODEL)),
            w_in_cols(D_MODEL, 4),
            const((D_MODEL, D_MODEL)),
            w_in_cols(2 * D_MODEL, 3),
            const((3, D_MODEL)),
            const((HEAD_DIM, 1)),
            const((1, HEAD_DIM)),
            const((D_MODEL, D_MODEL)),
            slab(2 * D_FF),
            slab(D_MODEL),
            slab(D_MODEL),
            wdn_slab,
        ],
        out_specs=[
            rows(D_MODEL),
            rows(D_MODEL),
            pl.BlockSpec((D_MODEL, PROJ_ROWS), lambda i: (0, i)),
            rows(D_MODEL),
            pl.BlockSpec((N_HEADS, 1, HEAD_DIM, MOBA_BLOCK), lambda i: (0, i, 0, 0)),
            pl.BlockSpec((N_HEADS, nb, PROJ_ROWS), lambda i: (0, 0, i)),
            slab(2 * D_FF),
            slab(D_MODEL),
            slab(D_MODEL),
            wdn_slab,
        ],
        out_shape=[
            jax.ShapeDtypeStruct((s, D_MODEL), _F32),
            jax.ShapeDtypeStruct((s, D_MODEL), _F32),
            jax.ShapeDtypeStruct((D_MODEL, s), _BF),
            jax.ShapeDtypeStruct((s, D_MODEL), _BF),
            jax.ShapeDtypeStruct((N_HEADS, nb, HEAD_DIM, MOBA_BLOCK), _BF),
            jax.ShapeDtypeStruct((N_HEADS, nb, s), _BF),
            jax.ShapeDtypeStruct((D_MODEL, 2 * D_FF), _BF),
            jax.ShapeDtypeStruct((D_MODEL, D_MODEL), _BF),
            jax.ShapeDtypeStruct((D_MODEL, D_MODEL), _BF),
            jax.ShapeDtypeStruct((D_FF, D_MODEL), _BF),
        ],
        scratch_shapes=[
            pltpu.VMEM((CONV_HALO_ROWS, D_MODEL), _F32),
            pltpu.VMEM((nb, D_MODEL), _F32),
        ],
        compiler_params=pltpu.CompilerParams(
            dimension_semantics=("arbitrary",), vmem_limit_bytes=VMEM_LIMIT),
        name="proj",
    )(x, nm, wb, wqt, wb, wvt, wb, cw, qn, kn, woa, w_up, w_out_b, w_o, w_down)


def _attn_kernel(qt_ref, kblk_ref, vtblk_ref, bias_ref, o_ref, k_ref, vt_ref, qaug_ref, sa_ref, sb_ref,
                 mxa_ref, mxb_ref, m_ref, l_ref, acc_ref, *, n_blocks):
    a = pl.program_id(1)
    own0 = Q_BLOCKS * a
    own1 = own0 + 1
    for b in range(Q_BLOCKS):
        k_ref[own0 + b] = kblk_ref[b]
        for hh in range(ATTN_HEADS):
            vt_ref[hh, own0 + b] = vtblk_ref[hh, b]
    ones_rows = jnp.ones((ONES_ROWS, MOBA_BLOCK), _BF)
    pad_rows = jnp.zeros((QK_DEPTH - HEAD_DIM - n_blocks, Q_TILE), _BF)
    lane = lax.broadcasted_iota(jnp.int32, (ONES_ROWS, HEAD_DIM), 1)

    def head_rows(hh):
        return slice(hh * HEAD_DIM, (hh + 1) * HEAD_DIM)

    def unit_blocks(u):
        js = [UNIT_BLOCKS * (u - 1) + b for b in range(UNIT_BLOCKS)]
        return [jnp.where(j < 0, own0 + b, jnp.where(j < own0, j, own1)) for b, j in enumerate(js)]

    def keys_with_onehot(j, hh, bias_row):
        row = (lane == bias_row).astype(_BF)
        onehot = jnp.concatenate([row] * (MOBA_BLOCK // ONES_ROWS), axis=0)
        return jnp.concatenate([k_ref[j, :, head_rows(hh)], onehot], axis=1)

    def put_scores(hh, s, dst_ref, mx_ref):
        dst_ref[hh] = s
        mx_ref[hh] = jnp.max(s, axis=0, keepdims=True)

    def scores(u, hh, dst_ref, mx_ref):
        ka = jnp.concatenate([keys_with_onehot(j, hh, j) for j in unit_blocks(u)], axis=0)
        put_scores(hh, _dot(ka, qaug_ref[hh]), dst_ref, mx_ref)

    def first_scores(hh, dst_ref, mx_ref):
        ka = jnp.concatenate([keys_with_onehot(own0, hh, n_blocks), keys_with_onehot(own1, hh, n_blocks)], axis=0)
        s = _dot(ka, qaug_ref[hh])
        kpos = lax.broadcasted_iota(jnp.int32, (MOBA_BLOCK, Q_TILE), 0)
        qcol = lax.broadcasted_iota(jnp.int32, (1, Q_TILE), 1)
        bias = bias_ref[hh].astype(_F32)
        blk = lax.broadcasted_iota(jnp.int32, bias.shape, 0)
        picked = jnp.max(jnp.where(blk == own0, bias, NEG_BIG), axis=0, keepdims=True) >= 0.0
        last0 = jnp.where(qcol < MOBA_BLOCK, qcol, jnp.where(picked, MOBA_BLOCK, -1))
        last1 = qcol - MOBA_BLOCK
        s = jnp.concatenate([jnp.where(kpos <= last0, s[:MOBA_BLOCK], NEG_BIG),
                             jnp.where(kpos <= last1, s[MOBA_BLOCK:], NEG_BIG)], axis=0)
        put_scores(hh, s, dst_ref, mx_ref)

    def update(u, hh, src_ref, mx_ref):
        m_old = m_ref[hh]
        m_new = jnp.maximum(m_old, mx_ref[hh])
        alpha = jnp.exp2(m_old - m_new)
        outs = []
        for b, j in enumerate(unit_blocks(u)):
            p = jnp.exp2(src_ref[hh, b * MOBA_BLOCK:(b + 1) * MOBA_BLOCK, :] - m_new).astype(_BF)
            outs.append(_dot(jnp.concatenate([vt_ref[hh, j], ones_rows], axis=0), p))
        out = functools.reduce(jnp.add, outs)
        m_ref[hh] = m_new
        l_ref[hh] = alpha * l_ref[hh] + out[HEAD_DIM:HEAD_DIM + 1]
        acc_ref[hh] = alpha * acc_ref[hh] + out[:HEAD_DIM]

    for hh in range(ATTN_HEADS):
        qaug_ref[hh] = jnp.concatenate([qt_ref[head_rows(hh), :], bias_ref[hh], pad_rows], axis=0)
        m_ref[hh] = jnp.full(m_ref.shape[1:], NEG_BIG, _F32)
        l_ref[hh] = jnp.zeros(l_ref.shape[1:], _F32)
        acc_ref[hh] = jnp.zeros(acc_ref.shape[1:], _F32)
    for hh in range(ATTN_HEADS):
        first_scores(hh, sa_ref, mxa_ref)

    def unit_pairs(u0, n_pairs):
        for pair in range(n_pairs):
            u = u0 + 2 * pair
            for hh in range(ATTN_HEADS):
                scores(u + 1, hh, sb_ref, mxb_ref)
                update(u, hh, sa_ref, mxa_ref)
            for hh in range(ATTN_HEADS):
                scores(u + 2, hh, sa_ref, mxa_ref)
                update(u + 1, hh, sb_ref, mxb_ref)

    n_pairs = lax.shift_right_logical(a + 1, 1)
    long_trips = lax.shift_right_logical(n_pairs, LONG_TRIP_SHIFT)

    def long_body(t, carry):
        unit_pairs(2 * LONG_TRIP_PAIRS * t, LONG_TRIP_PAIRS)
        return carry

    def short_body(t, carry):
        unit_pairs(2 * (LONG_TRIP_PAIRS * long_trips + t), 1)
        return carry

    lax.fori_loop(0, long_trips, long_body, 0)
    lax.fori_loop(0, n_pairs - LONG_TRIP_PAIRS * long_trips, short_body, 0)
    for hh in range(ATTN_HEADS):
        update(2 * n_pairs, hh, sa_ref, mxa_ref)
    for hh in range(ATTN_HEADS):
        o_ref[:, head_rows(hh)] = (acc_ref[hh] * (1.0 / l_ref[hh])).T.astype(o_ref.dtype)


def _attn(qt, k, vt, bias):
    s = k.shape[0]
    nb = s // MOBA_BLOCK
    assert nb < QK_DEPTH - HEAD_DIM and nb % Q_BLOCKS == 0 and UNIT_BLOCKS == Q_BLOCKS == 2
    k3 = k.reshape(nb, MOBA_BLOCK, D_MODEL)
    gw = ATTN_HEADS * HEAD_DIM
    scores_buf = pltpu.VMEM((ATTN_HEADS, UNIT_KEYS, Q_TILE), _F32)
    row_buf = pltpu.VMEM((ATTN_HEADS, 1, Q_TILE), _F32)
    return pl.pallas_call(
        functools.partial(_attn_kernel, n_blocks=nb),
        grid=(N_HEADS // ATTN_HEADS, nb // Q_BLOCKS),
        in_specs=[
            pl.BlockSpec((gw, Q_TILE), lambda g, a: (g, a)),
            pl.BlockSpec((Q_BLOCKS, MOBA_BLOCK, gw), lambda g, a: (a, 0, g)),
            pl.BlockSpec((ATTN_HEADS, Q_BLOCKS, HEAD_DIM, MOBA_BLOCK), lambda g, a: (g, a, 0, 0)),
            pl.BlockSpec((ATTN_HEADS, nb, Q_TILE), lambda g, a: (g, 0, a)),
        ],
        out_specs=pl.BlockSpec((Q_TILE, gw), lambda g, a: (a, g)),
        out_shape=jax.ShapeDtypeStruct((s, D_MODEL), _BF),
        scratch_shapes=[
            pltpu.VMEM((nb, MOBA_BLOCK, gw), _BF),
            pltpu.VMEM((ATTN_HEADS, nb, HEAD_DIM, MOBA_BLOCK), _BF),
            pltpu.VMEM((ATTN_HEADS, QK_DEPTH, Q_TILE), _BF),
            scores_buf,
            scores_buf,
            row_buf,
            row_buf,
            row_buf,
            row_buf,
            pltpu.VMEM((ATTN_HEADS, HEAD_DIM, Q_TILE), _F32),
        ],
        compiler_params=pltpu.CompilerParams(
            dimension_semantics=("arbitrary", "arbitrary"), vmem_limit_bytes=VMEM_LIMIT),
        name="moba_attn",
    )(qt, k3, vt, bias)


def _post_kernel(o_ref, sa_ref, sgb_ref, x_ref, wob_ref, wo_ref, nf_ref, wup_ref, fcw_ref, fcb_ref,
                 wdn_ref, out_ref, halo_ref, ubuf_ref):
    i = pl.program_id(0)

    @pl.when(i == 0)
    def _():
        halo_ref[...] = jnp.zeros_like(halo_ref)

    branch_b = _dot(o_ref[...], wob_ref[...])
    merged = sa_ref[...] + sgb_ref[...] * branch_b
    h = x_ref[...] + _dot(merged.astype(_BF), wo_ref[...])
    hn = (_rms(h) * nf_ref[...]).astype(_BF)

    n_chunks = D_FF // FFN_CHUNK

    def up_proj(c):
        for part in range(2):
            lo = part * D_FF + c * FFN_CHUNK
            u = _dot(hn, wup_ref[:, lo:lo + FFN_CHUNK])
            buf = ubuf_ref.at[c % (FFN_LOOKAHEAD + 1), part]
            buf[:CONV_HALO_ROWS, :] = halo_ref[:, lo:lo + FFN_CHUNK]
            buf[CONV_HALO_ROWS:, :] = u
            halo_ref[:, lo:lo + FFN_CHUNK] = u[POST_ROWS - CONV_HALO_ROWS:, :]

    def conv(c, part):
        lo = part * D_FF + c * FFN_CHUNK
        buf = ubuf_ref.at[c % (FFN_LOOKAHEAD + 1), part]
        w = fcw_ref[:, lo:lo + FFN_CHUNK]
        taps = [buf[CONV_HALO_ROWS - 2 + j:CONV_HALO_ROWS - 2 + j + POST_ROWS, :] for j in range(3)]
        return (w[0:1, :] * taps[0] + w[1:2, :] * taps[1] + w[2:3, :] * taps[2]
                + fcb_ref[:, lo:lo + FFN_CHUNK])

    acc = h
    for c in range(min(FFN_LOOKAHEAD, n_chunks)):
        up_proj(c)
    for c in range(n_chunks):
        if c + FFN_LOOKAHEAD < n_chunks:
            up_proj(c + FFN_LOOKAHEAD)
        act = (jax.nn.silu(conv(c, 0)) * conv(c, 1)).astype(_BF)
        acc = acc + _dot(act, wdn_ref[c * FFN_CHUNK:(c + 1) * FFN_CHUNK, :])
    out_ref[...] = acc


def _post(o, sa, sgb, x, wob, wo, nf, wup, fcw, fcb, wdn):
    s = x.shape[0]
    const = lambda shape: pl.BlockSpec(shape, lambda i: (0,) * len(shape), pipeline_mode=pl.Buffered(1))
    rows = pl.BlockSpec((POST_ROWS, D_MODEL), lambda i: (i, 0))
    return pl.pallas_call(
        _post_kernel,
        grid=(s // POST_ROWS,),
        in_specs=[
            rows, rows, rows, rows,
            const((D_MODEL, D_MODEL)),
            const((D_MODEL, D_MODEL)),
            const((1, D_MODEL)),
            const((D_MODEL, 2 * D_FF)),
            const((3, 2 * D_FF)),
            const((1, 2 * D_FF)),
            const((D_FF, D_MODEL)),
        ],
        out_specs=rows,
        out_shape=jax.ShapeDtypeStruct((s, D_MODEL), _F32),
        scratch_shapes=[
            pltpu.VMEM((CONV_HALO_ROWS, 2 * D_FF), _F32),
            pltpu.VMEM((FFN_LOOKAHEAD + 1, 2, CONV_HALO_ROWS + POST_ROWS, FFN_CHUNK), _F32),
        ],
        compiler_params=pltpu.CompilerParams(
            dimension_semantics=("arbitrary",), vmem_limit_bytes=VMEM_LIMIT),
        name="post",
    )(o, sa, sgb, x, wob, wo, nf, wup, fcw, fcb, wdn)


def kernel(x, norm_mix, w_in, conv_a_w, q_norm, k_norm, w_out_a, w_out_b, w_o, norm_ffn, w_up,
           ffn_conv_w, ffn_conv_b, w_down):
    b, s, d = x.shape
    assert b == 1 and d == D_MODEL and s % POST_ROWS == 0 and s % MOBA_BLOCK == 0
    x2 = x.reshape(s, d)
    wb = w_in.astype(_BF)
    wqt = wb[:, 3 * D_MODEL:4 * D_MODEL].T
    wvt = wb[:, 5 * D_MODEL:6 * D_MODEL].T

    sa, sgb, qt, k, vt, bias, wup, wob, wo, wdn = _proj(
        x2, norm_mix.reshape(1, d), wb, wqt, wvt, conv_a_w, q_norm.reshape(HEAD_DIM, 1),
        k_norm.reshape(1, HEAD_DIM), w_out_a.astype(_BF), w_up, w_out_b, w_o, w_down)
    o = _attn(qt, k, vt, bias)
    out = _post(o, sa, sgb, x2, wob, wo, norm_ffn.reshape(1, d), wup, ffn_conv_w,
                ffn_conv_b.reshape(1, 2 * D_FF), wdn)
    return out.reshape(b, s, d)
```

```python
import functools

import jax
import jax.numpy as jnp
from jax import lax
from jax.experimental import pallas as pl
from jax.experimental.pallas import tpu as pltpu

D_MODEL = 1024
N_HEADS = 8
HEAD_DIM = 128
MOBA_BLOCK = 256
MOBA_TOPK = 3
D_FF = 2816
EPS = 1e-6
CONV_HALO_ROWS = 8
NEG_BIG = -1e30

PROJ_ROWS = 256
WDN_CAST_STEPS = 16
POST_ROWS = 512
FFN_CHUNK = 256
FFN_LOOKAHEAD = 3
VMEM_LIMIT = 56 * 1024 * 1024
ATTN_HEADS = 4
ONES_ROWS = 16
UNIT_SHIFT = 1
UNIT_BLOCKS = 1 << UNIT_SHIFT
UNIT_KEYS = UNIT_BLOCKS * MOBA_BLOCK
Q_BLOCKS = 2
Q_TILE = Q_BLOCKS * MOBA_BLOCK
LONG_TRIP_SHIFT = 1
LONG_TRIP_PAIRS = 1 << LONG_TRIP_SHIFT
QK_DEPTH = 2 * HEAD_DIM
QK_SCALE = HEAD_DIM ** -0.5 * 1.4426950408889634

_BF = jnp.bfloat16
_F32 = jnp.float32


def _dot(a, b):
    return jnp.dot(a, b, preferred_element_type=_F32)


def _dot_nt(a, b):
    return lax.dot_general(a, b, (((1,), (1,)), ((), ())), preferred_element_type=_F32)


def _rms(x, axis=-1):
    return x * lax.rsqrt(jnp.mean(x * x, axis=axis, keepdims=True) + EPS)


def _causal_conv3(c, prev, w):
    row = lax.broadcasted_iota(jnp.int32, c.shape, 0)
    p1 = prev[CONV_HALO_ROWS - 1:CONV_HALO_ROWS, :]
    p2 = prev[CONV_HALO_ROWS - 2:CONV_HALO_ROWS - 1, :]
    c1 = jnp.where(row == 0, p1, pltpu.roll(c, 1, 0))
    c2 = jnp.where(row == 0, p2, jnp.where(row == 1, p1, pltpu.roll(c, 2, 0)))
    return w[0:1, :] * c2 + w[1:2, :] * c1 + w[2:3, :] * c


def _proj_kernel(x_ref, nm_ref, wc_ref, wqt_ref, wk_ref, wvt_ref, wg_ref, cw_ref, qn_ref, kn_ref, woa_ref,
                 wup32_ref, wob32_ref, wo32_ref, wdn32_ref,
                 sa_ref, sgb_ref, qt_ref, k_ref, vt_ref, bias_ref, wup_ref, wob_ref, wo_ref, wdn_ref,
                 halo_ref, kbar_ref):
    i = pl.program_id(0)

    @pl.when(i == 0)
    def _():
        halo_ref[...] = jnp.zeros_like(halo_ref)
        kbar_ref[...] = jnp.zeros_like(kbar_ref)

    wup_ref[...] = wup32_ref[...].astype(_BF)
    wob_ref[...] = wob32_ref[...].astype(_BF)
    wo_ref[...] = wo32_ref[...].astype(_BF)

    @pl.when(i < WDN_CAST_STEPS)
    def _():
        wdn_ref[...] = wdn32_ref[...].astype(_BF)

    xb = (_rms(x_ref[...]) * nm_ref[...]).astype(_BF)
    n_blocks = kbar_ref.shape[0]

    zk = _dot(xb, wk_ref[...])
    zqt = _dot_nt(wqt_ref[...], xb)
    zc = _dot(xb, wc_ref[...])

    gates = []
    for h in range(N_HEADS):
        lo = h * HEAD_DIM
        qh = (_rms(zqt[lo:lo + HEAD_DIM, :], axis=0) * qn_ref[...] * QK_SCALE).astype(_BF)
        qt_ref[lo:lo + HEAD_DIM, :] = qh
        gates.append(_dot(kbar_ref[:, lo:lo + HEAD_DIM].astype(_BF), qh))

    zg = _dot(xb, wg_ref[...])

    for h in range(N_HEADS):
        gate = gates[h]
        blk = lax.broadcasted_iota(jnp.int32, gate.shape, 0)
        gate = jnp.where(blk < i, gate, -jnp.inf)
        bias = jnp.full(gate.shape, NEG_BIG, _F32)
        for _ in range(MOBA_TOPK):
            top = jnp.max(gate, axis=0, keepdims=True)
            first = jnp.min(jnp.where(gate == top, blk, n_blocks), axis=0, keepdims=True)
            hit = blk == first
            bias = jnp.where(hit, 0.0, bias)
            gate = jnp.where(hit, -jnp.inf, gate)
        bias_ref[h] = jnp.where(blk < i, bias, NEG_BIG).astype(_BF)

    kbar_rows = []
    for h in range(N_HEADS):
        lo = h * HEAD_DIM
        kh = _rms(zk[:, lo:lo + HEAD_DIM]) * kn_ref[...]
        k_ref[:, lo:lo + HEAD_DIM] = kh.astype(_BF)
        kbar_rows.append(jnp.mean(kh, axis=0, keepdims=True))
    kbar_all = kbar_ref[...]
    blk_row = lax.broadcasted_iota(jnp.int32, kbar_all.shape, 0)
    kbar_ref[...] = jnp.where(blk_row == i, jnp.concatenate(kbar_rows, axis=1), kbar_all)

    bg = zc[:, :D_MODEL]
    c = zc[:, D_MODEL:2 * D_MODEL] * zc[:, 2 * D_MODEL:]
    y = _causal_conv3(c, halo_ref[...], cw_ref[...])
    halo_ref[...] = c[PROJ_ROWS - CONV_HALO_ROWS:, :]
    branch_a = _dot((bg * y).astype(_BF), woa_ref[...])

    vt = _dot_nt(wvt_ref[...], xb)

    sa_ref[...] = jax.nn.sigmoid(zg[:, :D_MODEL]) * branch_a
    sgb_ref[...] = jax.nn.sigmoid(zg[:, D_MODEL:])
    for h in range(N_HEADS):
        vt_ref[h, 0] = vt[h * HEAD_DIM:(h + 1) * HEAD_DIM, :].astype(_BF)


def _proj(x, nm, wb, wqt, wvt, cw, qn, kn, woa, w_up, w_out_b, w_o, w_down):
    s = x.shape[0]
    assert PROJ_ROWS == MOBA_BLOCK
    nb = s // PROJ_ROWS
    cast_rows = D_MODEL // nb
    wdn_rows = D_FF // WDN_CAST_STEPS
    assert cast_rows * nb == D_MODEL and cast_rows % 16 == 0 and wdn_rows % 16 == 0 and nb >= WDN_CAST_STEPS
    const = lambda shape: pl.BlockSpec(shape, lambda i: (0,) * len(shape), pipeline_mode=pl.Buffered(1))
    w_in_cols = lambda width, blk: pl.BlockSpec((D_MODEL, width), lambda i: (0, blk),
                                                pipeline_mode=pl.Buffered(1))
    rows = lambda w: pl.BlockSpec((PROJ_ROWS, w), lambda i: (i, 0))
    slab = lambda w: pl.BlockSpec((cast_rows, w), lambda i: (i, 0))
    wdn_slab = pl.BlockSpec((wdn_rows, D_MODEL), lambda i: (jnp.minimum(i, WDN_CAST_STEPS - 1), 0))
    return pl.pallas_call(
        _proj_kernel,
        grid=(nb,),
        in_specs=[
            rows(D_MODEL),
            const((1, D_MODEL)),
            w_in_cols(3 * D_MODEL, 0),
            const((D_MODEL, D_MODEL)),
            w_in_cols(D_MODEL, 4),
            const((D_MODEL, D_MODEL)),
            w_in_cols(2 * D_MODEL, 3),
            const((3, D_MODEL)),
            const((HEAD_DIM, 1)),
            const((1, HEAD_DIM)),
            const((D_MODEL, D_MODEL)),
            slab(2 * D_FF),
            slab(D_MODEL),
            slab(D_MODEL),
            wdn_slab,
        ],
        out_specs=[
            rows(D_MODEL),
            rows(D_MODEL),
            pl.BlockSpec((D_MODEL, PROJ_ROWS), lambda i: (0, i)),
            rows(D_MODEL),
            pl.BlockSpec((N_HEADS, 1, HEAD_DIM, MOBA_BLOCK), lambda i: (0, i, 0, 0)),
            pl.BlockSpec((N_HEADS, nb, PROJ_ROWS), lambda i: (0, 0, i)),
            slab(2 * D_FF),
            slab(D_MODEL),
            slab(D_MODEL),
            wdn_slab,
        ],
        out_shape=[
            jax.ShapeDtypeStruct((s, D_MODEL), _F32),
            jax.ShapeDtypeStruct((s, D_MODEL), _F32),
            jax.ShapeDtypeStruct((D_MODEL, s), _BF),
            jax.ShapeDtypeStruct((s, D_MODEL), _BF),
            jax.ShapeDtypeStruct((N_HEADS, nb, HEAD_DIM, MOBA_BLOCK), _BF),
            jax.ShapeDtypeStruct((N_HEADS, nb, s), _BF),
            jax.ShapeDtypeStruct((D_MODEL, 2 * D_FF), _BF),
            jax.ShapeDtypeStruct((D_MODEL, D_MODEL), _BF),
            jax.ShapeDtypeStruct((D_MODEL, D_MODEL), _BF),
            jax.ShapeDtypeStruct((D_FF, D_MODEL), _BF),
        ],
        scratch_shapes=[
            pltpu.VMEM((CONV_HALO_ROWS, D_MODEL), _F32),
            pltpu.VMEM((nb, D_MODEL), _F32),
        ],
        compiler_params=pltpu.CompilerParams(
            dimension_semantics=("arbitrary",), vmem_limit_bytes=VMEM_LIMIT),
        name="proj",
    )(x, nm, wb, wqt, wb, wvt, wb, cw, qn, kn, woa, w_up, w_out_b, w_o, w_down)


def _attn_kernel(qt_ref, kblk_ref, vtblk_ref, bias_ref, o_ref, k_ref, vt_ref, qaug_ref, sa_ref, sb_ref,
                 mxa_ref, mxb_ref, m_ref, l_ref, acc_ref, *, n_blocks):
    a = pl.program_id(1)
    own0 = Q_BLOCKS * a
    own1 = own0 + 1
    for b in range(Q_BLOCKS):
        k_ref[own0 + b] = kblk_ref[b]
        for hh in range(ATTN_HEADS):
            vt_ref[hh, own0 + b] = vtblk_ref[hh, b]
    ones_rows = jnp.ones((ONES_ROWS, MOBA_BLOCK), _BF)
    pad_rows = jnp.zeros((QK_DEPTH - HEAD_DIM - n_blocks, Q_TILE), _BF)
    lane = lax.broadcasted_iota(jnp.int32, (ONES_ROWS, HEAD_DIM), 1)

    def head_rows(hh):
        return slice(hh * HEAD_DIM, (hh + 1) * HEAD_DIM)

    def unit_blocks(u):
        js = [UNIT_BLOCKS * (u - 1) + b for b in range(UNIT_BLOCKS)]
        return [jnp.where(j < own0, j, own1) for j in js]

    def keys_with_onehot(j, hh, bias_row):
        row = (lane == bias_row).astype(_BF)
        onehot = jnp.concatenate([row] * (MOBA_BLOCK // ONES_ROWS), axis=0)
        return jnp.concatenate([k_ref[j, :, head_rows(hh)], onehot], axis=1)

    def put_scores(hh, s, dst_ref, mx_ref):
        dst_ref[hh] = s
        mx_ref[hh] = jnp.max(s, axis=0, keepdims=True)

    def scores(u, hh, dst_ref, mx_ref):
        ka = jnp.concatenate([keys_with_onehot(j, hh, j) for j in unit_blocks(u)], axis=0)
        put_scores(hh, _dot(ka, qaug_ref[hh]), dst_ref, mx_ref)

    def first_scores(hh, dst_ref, mx_ref):
        ka = jnp.concatenate([keys_with_onehot(own0, hh, n_blocks), keys_with_onehot(own1, hh, n_blocks)], axis=0)
        s = _dot(ka, qaug_ref[hh])
        kpos = lax.broadcasted_iota(jnp.int32, (MOBA_BLOCK, Q_TILE), 0)
        qcol = lax.broadcasted_iota(jnp.int32, (1, Q_TILE), 1)
        bias = bias_ref[hh].astype(_F32)
        blk = lax.broadcasted_iota(jnp.int32, bias.shape, 0)
        picked = jnp.max(jnp.where(blk == own0, bias, NEG_BIG), axis=0, keepdims=True) >= 0.0
        last0 = jnp.where(qcol < MOBA_BLOCK, qcol, jnp.where(picked, MOBA_BLOCK, -1))
        last1 = qcol - MOBA_BLOCK
        s = jnp.concatenate([jnp.where(kpos <= last0, s[:MOBA_BLOCK], NEG_BIG),
                             jnp.where(kpos <= last1, s[MOBA_BLOCK:], NEG_BIG)], axis=0)
        put_scores(hh, s, dst_ref, mx_ref)

    def update(blocks, hh, src_ref, mx_ref):
        m_old = m_ref[hh]
        m_new = jnp.maximum(m_old, mx_ref[hh])
        alpha = jnp.exp2(m_old - m_new)
        outs = []
        for b, j in enumerate(blocks):
            p = jnp.exp2(src_ref[hh, b * MOBA_BLOCK:(b + 1) * MOBA_BLOCK, :] - m_new).astype(_BF)
            outs.append(_dot(jnp.concatenate([vt_ref[hh, j], ones_rows], axis=0), p))
        out = functools.reduce(jnp.add, outs)
        m_ref[hh] = m_new
        l_ref[hh] = alpha * l_ref[hh] + out[HEAD_DIM:HEAD_DIM + 1]
        acc_ref[hh] = alpha * acc_ref[hh] + out[:HEAD_DIM]

    for hh in range(ATTN_HEADS):
        qaug_ref[hh] = jnp.concatenate([qt_ref[head_rows(hh), :], bias_ref[hh], pad_rows], axis=0)
        m_ref[hh] = jnp.full(m_ref.shape[1:], NEG_BIG, _F32)
        l_ref[hh] = jnp.zeros(l_ref.shape[1:], _F32)
        acc_ref[hh] = jnp.zeros(acc_ref.shape[1:], _F32)
    for hh in range(ATTN_HEADS):
        first_scores(hh, sa_ref, mxa_ref)

    def unit_pairs(u0, n_pairs, first=False):
        for pair in range(n_pairs):
            u = u0 + 2 * pair
            for hh in range(ATTN_HEADS):
                scores(u + 1, hh, sb_ref, mxb_ref)
                update([own0, own1] if first and pair == 0 else unit_blocks(u), hh, sa_ref, mxa_ref)
            for hh in range(ATTN_HEADS):
                scores(u + 2, hh, sa_ref, mxa_ref)
                update(unit_blocks(u + 1), hh, sb_ref, mxb_ref)

    unit_pairs(0, 1, first=True)
    n_units = a + 1
    more_pairs = lax.shift_right_logical(n_units + 1, 1) - 1
    long_trips = lax.shift_right_logical(more_pairs, LONG_TRIP_SHIFT)

    def long_body(t, carry):
        unit_pairs(2 + 2 * LONG_TRIP_PAIRS * t, LONG_TRIP_PAIRS)
        return carry

    def short_body(t, carry):
        unit_pairs(2 + 2 * (LONG_TRIP_PAIRS * long_trips + t), 1)
        return carry

    lax.fori_loop(0, long_trips, long_body, 0)
    lax.fori_loop(0, more_pairs - LONG_TRIP_PAIRS * long_trips, short_body, 0)
    for hh in range(ATTN_HEADS):
        o_ref[:, head_rows(hh)] = (acc_ref[hh] * (1.0 / l_ref[hh])).T.astype(o_ref.dtype)


def _attn(qt, k, vt, bias):
    s = k.shape[0]
    nb = s // MOBA_BLOCK
    assert nb < QK_DEPTH - HEAD_DIM and nb % Q_BLOCKS == 0 and UNIT_BLOCKS == Q_BLOCKS == 2
    k3 = k.reshape(nb, MOBA_BLOCK, D_MODEL)
    gw = ATTN_HEADS * HEAD_DIM
    scores_buf = pltpu.VMEM((ATTN_HEADS, UNIT_KEYS, Q_TILE), _F32)
    row_buf = pltpu.VMEM((ATTN_HEADS, 1, Q_TILE), _F32)
    return pl.pallas_call(
        functools.partial(_attn_kernel, n_blocks=nb),
        grid=(N_HEADS // ATTN_HEADS, nb // Q_BLOCKS),
        in_specs=[
            pl.BlockSpec((gw, Q_TILE), lambda g, a: (g, a)),
            pl.BlockSpec((Q_BLOCKS, MOBA_BLOCK, gw), lambda g, a: (a, 0, g)),
            pl.BlockSpec((ATTN_HEADS, Q_BLOCKS, HEAD_DIM, MOBA_BLOCK), lambda g, a: (g, a, 0, 0)),
            pl.BlockSpec((ATTN_HEADS, nb, Q_TILE), lambda g, a: (g, 0, a)),
        ],
        out_specs=pl.BlockSpec((Q_TILE, gw), lambda g, a: (a, g)),
        out_shape=jax.ShapeDtypeStruct((s, D_MODEL), _BF),
        scratch_shapes=[
            pltpu.VMEM((nb, MOBA_BLOCK, gw), _BF),
            pltpu.VMEM((ATTN_HEADS, nb, HEAD_DIM, MOBA_BLOCK), _BF),
            pltpu.VMEM((ATTN_HEADS, QK_DEPTH, Q_TILE), _BF),
            scores_buf,
            scores_buf,
            row_buf,
            row_buf,
            row_buf,
            row_buf,
            pltpu.VMEM((ATTN_HEADS, HEAD_DIM, Q_TILE), _F32),
        ],
        compiler_params=pltpu.CompilerParams(
            dimension_semantics=("arbitrary", "arbitrary"), vmem_limit_bytes=VMEM_LIMIT),
        name="moba_attn",
    )(qt, k3, vt, bias)


def _post_kernel(o_ref, sa_ref, sgb_ref, x_ref, wob_ref, wo_ref, nf_ref, wup_ref, fcw_ref, fcb_ref,
                 wdn_ref, out_ref, halo_ref, ubuf_ref):
    i = pl.program_id(0)

    @pl.when(i == 0)
    def _():
        halo_ref[...] = jnp.zeros_like(halo_ref)

    branch_b = _dot(o_ref[...], wob_ref[...])
    merged = sa_ref[...] + sgb_ref[...] * branch_b
    h = x_ref[...] + _dot(merged.astype(_BF), wo_ref[...])
    hn = (_rms(h) * nf_ref[...]).astype(_BF)

    n_chunks = D_FF // FFN_CHUNK

    def up_proj(c):
        for part in range(2):
            lo = part * D_FF + c * FFN_CHUNK
            u = _dot(hn, wup_ref[:, lo:lo + FFN_CHUNK])
            buf = ubuf_ref.at[c % (FFN_LOOKAHEAD + 1), part]
            buf[:CONV_HALO_ROWS, :] = halo_ref[:, lo:lo + FFN_CHUNK]
            buf[CONV_HALO_ROWS:, :] = u
            halo_ref[:, lo:lo + FFN_CHUNK] = u[POST_ROWS - CONV_HALO_ROWS:, :]

    def conv(c, part):
        lo = part * D_FF + c * FFN_CHUNK
        buf = ubuf_ref.at[c % (FFN_LOOKAHEAD + 1), part]
        w = fcw_ref[:, lo:lo + FFN_CHUNK]
        taps = [buf[CONV_HALO_ROWS - 2 + j:CONV_HALO_ROWS - 2 + j + POST_ROWS, :] for j in range(3)]
        return (w[0:1, :] * taps[0] + w[1:2, :] * taps[1] + w[2:3, :] * taps[2]
                + fcb_ref[:, lo:lo + FFN_CHUNK])

    acc = h
    for c in range(min(FFN_LOOKAHEAD, n_chunks)):
        up_proj(c)
    for c in range(n_chunks):
        if c + FFN_LOOKAHEAD < n_chunks:
            up_proj(c + FFN_LOOKAHEAD)
        act = (jax.nn.silu(conv(c, 0)) * conv(c, 1)).astype(_BF)
        acc = acc + _dot(act, wdn_ref[c * FFN_CHUNK:(c + 1) * FFN_CHUNK, :])
    out_ref[...] = acc


def _post(o, sa, sgb, x, wob, wo, nf, wup, fcw, fcb, wdn):
    s = x.shape[0]
    const = lambda shape: pl.BlockSpec(shape, lambda i: (0,) * len(shape), pipeline_mode=pl.Buffered(1))
    rows = pl.BlockSpec((POST_ROWS, D_MODEL), lambda i: (i, 0))
    return pl.pallas_call(
        _post_kernel,
        grid=(s // POST_ROWS,),
        in_specs=[
            rows, rows, rows, rows,
            const((D_MODEL, D_MODEL)),
            const((D_MODEL, D_MODEL)),
            const((1, D_MODEL)),
            const((D_MODEL, 2 * D_FF)),
            const((3, 2 * D_FF)),
            const((1, 2 * D_FF)),
            const((D_FF, D_MODEL)),
        ],
        out_specs=rows,
        out_shape=jax.ShapeDtypeStruct((s, D_MODEL), _F32),
        scratch_shapes=[
            pltpu.VMEM((CONV_HALO_ROWS, 2 * D_FF), _F32),
            pltpu.VMEM((FFN_LOOKAHEAD + 1, 2, CONV_HALO_ROWS + POST_ROWS, FFN_CHUNK), _F32),
        ],
        compiler_params=pltpu.CompilerParams(
            dimension_semantics=("arbitrary",), vmem_limit_bytes=VMEM_LIMIT),
        name="post",
    )(o, sa, sgb, x, wob, wo, nf, wup, fcw, fcb, wdn)


def kernel(x, norm_mix, w_in, conv_a_w, q_norm, k_norm, w_out_a, w_out_b, w_o, norm_ffn, w_up,
           ffn_conv_w, ffn_conv_b, w_down):
    b, s, d = x.shape
    assert b == 1 and d == D_MODEL and s % POST_ROWS == 0 and s % MOBA_BLOCK == 0
    x2 = x.reshape(s, d)
    wb = w_in.astype(_BF)
    wqt = wb[:, 3 * D_MODEL:4 * D_MODEL].T
    wvt = wb[:, 5 * D_MODEL:6 * D_MODEL].T

    sa, sgb, qt, k, vt, bias, wup, wob, wo, wdn = _proj(
        x2, norm_mix.reshape(1, d), wb, wqt, wvt, conv_a_w, q_norm.reshape(HEAD_DIM, 1),
        k_norm.reshape(1, HEAD_DIM), w_out_a.astype(_BF), w_up, w_out_b, w_o, w_down)
    o = _attn(qt, k, vt, bias)
    out = _post(o, sa, sgb, x2, wob, wo, norm_ffn.reshape(1, d), wup, ffn_conv_w,
                ffn_conv_b.reshape(1, 2 * D_FF), wdn)
    return out.reshape(b, s, d)
```

```python
import functools

import jax
import jax.numpy as jnp
from jax import lax
from jax.experimental import pallas as pl
from jax.experimental.pallas import tpu as pltpu

D_MODEL = 1024
N_HEADS = 8
HEAD_DIM = 128
MOBA_BLOCK = 256
MOBA_TOPK = 3
D_FF = 2816
EPS = 1e-6
CONV_HALO_ROWS = 8
NEG_BIG = -1e30

PROJ_ROWS = 256
WDN_CAST_STEPS = 16
POST_ROWS = 512
FFN_CHUNK = 256
FFN_LOOKAHEAD = 3
VMEM_LIMIT = 56 * 1024 * 1024
ATTN_HEADS = 4
ONES_ROWS = 16
UNIT_SHIFT = 1
UNIT_BLOCKS = 1 << UNIT_SHIFT
UNIT_KEYS = UNIT_BLOCKS * MOBA_BLOCK
Q_BLOCKS = 2
Q_TILE = Q_BLOCKS * MOBA_BLOCK
LONG_TRIP_SHIFT = 1
LONG_TRIP_PAIRS = 1 << LONG_TRIP_SHIFT
QK_DEPTH = 2 * HEAD_DIM
QK_SCALE = HEAD_DIM ** -0.5 * 1.4426950408889634

_BF = jnp.bfloat16
_F32 = jnp.float32


def _dot(a, b):
    return jnp.dot(a, b, preferred_element_type=_F32)


def _dot_nt(a, b):
    return lax.dot_general(a, b, (((1,), (1,)), ((), ())), preferred_element_type=_F32)


def _rms(x, axis=-1):
    return x * lax.rsqrt(jnp.mean(x * x, axis=axis, keepdims=True) + EPS)


def _causal_conv3(c, prev, w):
    row = lax.broadcasted_iota(jnp.int32, c.shape, 0)
    p1 = prev[CONV_HALO_ROWS - 1:CONV_HALO_ROWS, :]
    p2 = prev[CONV_HALO_ROWS - 2:CONV_HALO_ROWS - 1, :]
    c1 = jnp.where(row == 0, p1, pltpu.roll(c, 1, 0))
    c2 = jnp.where(row == 0, p2, jnp.where(row == 1, p1, pltpu.roll(c, 2, 0)))
    return w[0:1, :] * c2 + w[1:2, :] * c1 + w[2:3, :] * c


def _proj_kernel(x_ref, nm_ref, wc_ref, wqt_ref, wk_ref, wvt_ref, wg_ref, cw_ref, qn_ref, kn_ref, woa_ref,
                 wup32_ref, wob32_ref, wo32_ref, wdn32_ref,
                 sa_ref, sgb_ref, qt_ref, k_ref, vt_ref, bias_ref, wup_ref, wob_ref, wo_ref, wdn_ref,
                 halo_ref, kbar_ref):
    i = pl.program_id(0)

    @pl.when(i == 0)
    def _():
        halo_ref[...] = jnp.zeros_like(halo_ref)
        kbar_ref[...] = jnp.zeros_like(kbar_ref)

    xb = (_rms(x_ref[...]) * nm_ref[...]).astype(_BF)
    n_blocks = kbar_ref.shape[0]

    zk = _dot(xb, wk_ref[...])
    zqt = _dot_nt(wqt_ref[...], xb)
    zc = _dot(xb, wc_ref[...])

    gates = []
    for h in range(N_HEADS):
        lo = h * HEAD_DIM
        qh = (_rms(zqt[lo:lo + HEAD_DIM, :], axis=0) * qn_ref[...] * QK_SCALE).astype(_BF)
        qt_ref[lo:lo + HEAD_DIM, :] = qh
        gates.append(_dot(kbar_ref[:, lo:lo + HEAD_DIM].astype(_BF), qh))

    zg = _dot(xb, wg_ref[...])

    for h in range(N_HEADS):
        gate = gates[h]
        blk = lax.broadcasted_iota(jnp.int32, gate.shape, 0)
        gate = jnp.where(blk < i, gate, -jnp.inf)
        bias = jnp.full(gate.shape, NEG_BIG, _F32)
        for _ in range(MOBA_TOPK):
            top = jnp.max(gate, axis=0, keepdims=True)
            first = jnp.min(jnp.where(gate == top, blk, n_blocks), axis=0, keepdims=True)
            hit = blk == first
            bias = jnp.where(hit, 0.0, bias)
            gate = jnp.where(hit, -jnp.inf, gate)
        bias_ref[h] = jnp.where(blk < i, bias, NEG_BIG).astype(_BF)

    kbar_rows = []
    for h in range(N_HEADS):
        lo = h * HEAD_DIM
        kh = _rms(zk[:, lo:lo + HEAD_DIM]) * kn_ref[...]
        k_ref[:, lo:lo + HEAD_DIM] = kh.astype(_BF)
        kbar_rows.append(jnp.mean(kh, axis=0, keepdims=True))
    kbar_all = kbar_ref[...]
    blk_row = lax.broadcasted_iota(jnp.int32, kbar_all.shape, 0)
    kbar_ref[...] = jnp.where(blk_row == i, jnp.concatenate(kbar_rows, axis=1), kbar_all)

    bg = zc[:, :D_MODEL]
    c = zc[:, D_MODEL:2 * D_MODEL] * zc[:, 2 * D_MODEL:]
    y = _causal_conv3(c, halo_ref[...], cw_ref[...])
    halo_ref[...] = c[PROJ_ROWS - CONV_HALO_ROWS:, :]
    branch_a = _dot((bg * y).astype(_BF), woa_ref[...])

    vt = _dot_nt(wvt_ref[...], xb)

    sa_ref[...] = jax.nn.sigmoid(zg[:, :D_MODEL]) * branch_a
    sgb_ref[...] = jax.nn.sigmoid(zg[:, D_MODEL:])
    for h in range(N_HEADS):
        vt_ref[h, 0] = vt[h * HEAD_DIM:(h + 1) * HEAD_DIM, :].astype(_BF)

    wup_ref[...] = wup32_ref[...].astype(_BF)
    wob_ref[...] = wob32_ref[...].astype(_BF)
    wo_ref[...] = wo32_ref[...].astype(_BF)

    @pl.when(i < WDN_CAST_STEPS)
    def _():
        wdn_ref[...] = wdn32_ref[...].astype(_BF)


def _proj(x, nm, wb, wqt, wvt, cw, qn, kn, woa, w_up, w_out_b, w_o, w_down):
    s = x.shape[0]
    assert PROJ_ROWS == MOBA_BLOCK
    nb = s // PROJ_ROWS
    cast_rows = D_MODEL // nb
    wdn_rows = D_FF // WDN_CAST_STEPS
    assert cast_rows * nb == D_MODEL and cast_rows % 16 == 0 and wdn_rows % 16 == 0 and nb >= WDN_CAST_STEPS
    const = lambda shape: pl.BlockSpec(shape, lambda i: (0,) * len(shape), pipeline_mode=pl.Buffered(1))
    w_in_cols = lambda width, blk: pl.BlockSpec((D_MODEL, width), lambda i: (0, blk),
                                                pipeline_mode=pl.Buffered(1))
    rows = lambda w: pl.BlockSpec((PROJ_ROWS, w), lambda i: (i, 0))
    slab = lambda w: pl.BlockSpec((cast_rows, w), lambda i: (i, 0))
    wdn_slab = pl.BlockSpec((wdn_rows, D_MODEL), lambda i: (jnp.minimum(i, WDN_CAST_STEPS - 1), 0))
    return pl.pallas_call(
        _proj_kernel,
        grid=(nb,),
        in_specs=[
            rows(D_MODEL),
            const((1, D_MODEL)),
            w_in_cols(3 * D_MODEL, 0),
            const((D_MODEL, D_MODEL)),
            w_in_cols(D_MODEL, 4),
            const((D_MODEL, D_MODEL)),
            w_in_cols(2 * D_MODEL, 3),
            const((3, D_MODEL)),
            const((HEAD_DIM, 1)),
            const((1, HEAD_DIM)),
            const((D_MODEL, D_MODEL)),
            slab(2 * D_FF),
            slab(D_MODEL),
            slab(D_MODEL),
            wdn_slab,
        ],
        out_specs=[
            rows(D_MODEL),
            rows(D_MODEL),
            pl.BlockSpec((D_MODEL, PROJ_ROWS), lambda i: (0, i)),
            rows(D_MODEL),
            pl.BlockSpec((N_HEADS, 1, HEAD_DIM, MOBA_BLOCK), lambda i: (0, i, 0, 0)),
            pl.BlockSpec((N_HEADS, nb, PROJ_ROWS), lambda i: (0, 0, i)),
            slab(2 * D_FF),
            slab(D_MODEL),
            slab(D_MODEL),
            wdn_slab,
        ],
        out_shape=[
            jax.ShapeDtypeStruct((s, D_MODEL), _F32),
            jax.ShapeDtypeStruct((s, D_MODEL), _F32),
            jax.ShapeDtypeStruct((D_MODEL, s), _BF),
            jax.ShapeDtypeStruct((s, D_MODEL), _BF),
            jax.ShapeDtypeStruct((N_HEADS, nb, HEAD_DIM, MOBA_BLOCK), _BF),
            jax.ShapeDtypeStruct((N_HEADS, nb, s), _BF),
            jax.ShapeDtypeStruct((D_MODEL, 2 * D_FF), _BF),
            jax.ShapeDtypeStruct((D_MODEL, D_MODEL), _BF),
            jax.ShapeDtypeStruct((D_MODEL, D_MODEL), _BF),
            jax.ShapeDtypeStruct((D_FF, D_MODEL), _BF),
        ],
        scratch_shapes=[
            pltpu.VMEM((CONV_HALO_ROWS, D_MODEL), _F32),
            pltpu.VMEM((nb, D_MODEL), _F32),
        ],
        compiler_params=pltpu.CompilerParams(
            dimension_semantics=("arbitrary",), vmem_limit_bytes=VMEM_LIMIT),
        name="proj",
    )(x, nm, wb, wqt, wb, wvt, wb, cw, qn, kn, woa, w_up, w_out_b, w_o, w_down)


def _attn_kernel(qt_ref, kblk_ref, vtblk_ref, bias_ref, o_ref, k_ref, vt_ref, qaug_ref, sa_ref, sb_ref,
                 mxa_ref, mxb_ref, m_ref, l_ref, acc_ref, *, n_blocks):
    a = pl.program_id(1)
    own0 = Q_BLOCKS * a
    own1 = own0 + 1
    for b in range(Q_BLOCKS):
        k_ref[own0 + b] = kblk_ref[b]
        for hh in range(ATTN_HEADS):
            vt_ref[hh, own0 + b] = vtblk_ref[hh, b]
    ones_rows = jnp.ones((ONES_ROWS, MOBA_BLOCK), _BF)
    pad_rows = jnp.zeros((QK_DEPTH - HEAD_DIM - n_blocks, Q_TILE), _BF)
    lane = lax.broadcasted_iota(jnp.int32, (ONES_ROWS, HEAD_DIM), 1)

    def head_rows(hh):
        return slice(hh * HEAD_DIM, (hh + 1) * HEAD_DIM)

    def unit_blocks(u):
        js = [UNIT_BLOCKS * (u - 1) + b for b in range(UNIT_BLOCKS)]
        return [jnp.where(j < own0, j, own1) for j in js]

    def keys_with_onehot(j, hh, bias_row):
        row = (lane == bias_row).astype(_BF)
        onehot = jnp.concatenate([row] * (MOBA_BLOCK // ONES_ROWS), axis=0)
        return jnp.concatenate([k_ref[j, :, head_rows(hh)], onehot], axis=1)

    def put_scores(hh, s, dst_ref, mx_ref):
        dst_ref[hh] = s
        mx_ref[hh] = jnp.max(s, axis=0, keepdims=True)

    def scores(u, hh, dst_ref, mx_ref):
        ka = jnp.concatenate([keys_with_onehot(j, hh, j) for j in unit_blocks(u)], axis=0)
        put_scores(hh, _dot(ka, qaug_ref[hh]), dst_ref, mx_ref)

    def first_scores(hh, dst_ref, mx_ref):
        ka = jnp.concatenate([keys_with_onehot(own0, hh, n_blocks), keys_with_onehot(own1, hh, n_blocks)], axis=0)
        s = _dot(ka, qaug_ref[hh])
        kpos = lax.broadcasted_iota(jnp.int32, (MOBA_BLOCK, Q_TILE), 0)
        qcol = lax.broadcasted_iota(jnp.int32, (1, Q_TILE), 1)
        bias = bias_ref[hh].astype(_F32)
        blk = lax.broadcasted_iota(jnp.int32, bias.shape, 0)
        picked = jnp.max(jnp.where(blk == own0, bias, NEG_BIG), axis=0, keepdims=True) >= 0.0
        last0 = jnp.where(qcol < MOBA_BLOCK, qcol, jnp.where(picked, MOBA_BLOCK, -1))
        last1 = qcol - MOBA_BLOCK
        s = jnp.concatenate([jnp.where(kpos <= last0, s[:MOBA_BLOCK], NEG_BIG),
                             jnp.where(kpos <= last1, s[MOBA_BLOCK:], NEG_BIG)], axis=0)
        put_scores(hh, s, dst_ref, mx_ref)

    def update(blocks, hh, src_ref, mx_ref):
        m_old = m_ref[hh]
        m_new = jnp.maximum(m_old, mx_ref[hh])
        alpha = jnp.exp2(m_old - m_new)
        outs = []
        for b, j in enumerate(blocks):
            p = jnp.exp2(src_ref[hh, b * MOBA_BLOCK:(b + 1) * MOBA_BLOCK, :] - m_new).astype(_BF)
            outs.append(_dot(jnp.concatenate([vt_ref[hh, j], ones_rows], axis=0), p))
        out = functools.reduce(jnp.add, outs)
        m_ref[hh] = m_new
        l_ref[hh] = alpha * l_ref[hh] + out[HEAD_DIM:HEAD_DIM + 1]
        acc_ref[hh] = alpha * acc_ref[hh] + out[:HEAD_DIM]

    for hh in range(ATTN_HEADS):
        qaug_ref[hh] = jnp.concatenate([qt_ref[head_rows(hh), :], bias_ref[hh], pad_rows], axis=0)
        m_ref[hh] = jnp.full(m_ref.shape[1:], NEG_BIG, _F32)
        l_ref[hh] = jnp.zeros(l_ref.shape[1:], _F32)
        acc_ref[hh] = jnp.zeros(acc_ref.shape[1:], _F32)
    for hh in range(ATTN_HEADS):
        first_scores(hh, sa_ref, mxa_ref)

    def unit_pairs(u0, n_pairs, first=False):
        for pair in range(n_pairs):
            u = u0 + 2 * pair
            for hh in range(ATTN_HEADS):
                scores(u + 1, hh, sb_ref, mxb_ref)
                update([own0, own1] if first and pair == 0 else unit_blocks(u), hh, sa_ref, mxa_ref)
            for hh in range(ATTN_HEADS):
                scores(u + 2, hh, sa_ref, mxa_ref)
                update(unit_blocks(u + 1), hh, sb_ref, mxb_ref)

    unit_pairs(0, 1, first=True)
    left = jnp.maximum(a - 1, 0)
    more_pairs = lax.shift_right_logical(left, 1)
    long_trips = lax.shift_right_logical(more_pairs, LONG_TRIP_SHIFT)

    def long_body(t, carry):
        unit_pairs(2 + 2 * LONG_TRIP_PAIRS * t, LONG_TRIP_PAIRS)
        return carry

    def short_body(t, carry):
        unit_pairs(2 + 2 * (LONG_TRIP_PAIRS * long_trips + t), 1)
        return carry

    def last_unit_body(t, carry):
        for hh in range(ATTN_HEADS):
            update(unit_blocks(2 + 2 * more_pairs), hh, sa_ref, mxa_ref)
        return carry

    lax.fori_loop(0, long_trips, long_body, 0)
    lax.fori_loop(0, more_pairs - LONG_TRIP_PAIRS * long_trips, short_body, 0)
    lax.fori_loop(0, left - 2 * more_pairs, last_unit_body, 0)
    for hh in range(ATTN_HEADS):
        o_ref[:, head_rows(hh)] = (acc_ref[hh] * (1.0 / l_ref[hh])).T.astype(o_ref.dtype)


def _attn(qt, k, vt, bias):
    s = k.shape[0]
    nb = s // MOBA_BLOCK
    assert nb < QK_DEPTH - HEAD_DIM and nb % Q_BLOCKS == 0 and UNIT_BLOCKS == Q_BLOCKS == 2
    k3 = k.reshape(nb, MOBA_BLOCK, D_MODEL)
    gw = ATTN_HEADS * HEAD_DIM
    scores_buf = pltpu.VMEM((ATTN_HEADS, UNIT_KEYS, Q_TILE), _F32)
    row_buf = pltpu.VMEM((ATTN_HEADS, 1, Q_TILE), _F32)
    return pl.pallas_call(
        functools.partial(_attn_kernel, n_blocks=nb),
        grid=(N_HEADS // ATTN_HEADS, nb // Q_BLOCKS),
        in_specs=[
            pl.BlockSpec((gw, Q_TILE), lambda g, a: (g, a)),
            pl.BlockSpec((Q_BLOCKS, MOBA_BLOCK, gw), lambda g, a: (a, 0, g)),
            pl.BlockSpec((ATTN_HEADS, Q_BLOCKS, HEAD_DIM, MOBA_BLOCK), lambda g, a: (g, a, 0, 0)),
            pl.BlockSpec((ATTN_HEADS, nb, Q_TILE), lambda g, a: (g, 0, a)),
        ],
        out_specs=pl.BlockSpec((Q_TILE, gw), lambda g, a: (a, g)),
        out_shape=jax.ShapeDtypeStruct((s, D_MODEL), _BF),
        scratch_shapes=[
            pltpu.VMEM((nb, MOBA_BLOCK, gw), _BF),
            pltpu.VMEM((ATTN_HEADS, nb, HEAD_DIM, MOBA_BLOCK), _BF),
            pltpu.VMEM((ATTN_HEADS, QK_DEPTH, Q_TILE), _BF),
            scores_buf,
            scores_buf,
            row_buf,
            row_buf,
            row_buf,
            row_buf,
            pltpu.VMEM((ATTN_HEADS, HEAD_DIM, Q_TILE), _F32),
        ],
        compiler_params=pltpu.CompilerParams(
            dimension_semantics=("arbitrary", "arbitrary"), vmem_limit_bytes=VMEM_LIMIT),
        name="moba_attn",
    )(qt, k3, vt, bias)


def _post_kernel(o_ref, sa_ref, sgb_ref, x_ref, wob_ref, wo_ref, nf_ref, wup_ref, fcw_ref, fcb_ref,
                 wdn_ref, out_ref, halo_ref, ubuf_ref):
    i = pl.program_id(0)

    @pl.when(i == 0)
    def _():
        halo_ref[...] = jnp.zeros_like(halo_ref)

    branch_b = _dot(o_ref[...], wob_ref[...])
    merged = sa_ref[...] + sgb_ref[...] * branch_b
    h = x_ref[...] + _dot(merged.astype(_BF), wo_ref[...])
    hn = (_rms(h) * nf_ref[...]).astype(_BF)

    n_chunks = D_FF // FFN_CHUNK

    def up_proj(c):
        for part in range(2):
            lo = part * D_FF + c * FFN_CHUNK
            u = _dot(hn, wup_ref[:, lo:lo + FFN_CHUNK])
            buf = ubuf_ref.at[c % (FFN_LOOKAHEAD + 1), part]
            buf[:CONV_HALO_ROWS, :] = halo_ref[:, lo:lo + FFN_CHUNK]
            buf[CONV_HALO_ROWS:, :] = u
            halo_ref[:, lo:lo + FFN_CHUNK] = u[POST_ROWS - CONV_HALO_ROWS:, :]

    def conv(c, part):
        lo = part * D_FF + c * FFN_CHUNK
        buf = ubuf_ref.at[c % (FFN_LOOKAHEAD + 1), part]
        w = fcw_ref[:, lo:lo + FFN_CHUNK]
        taps = [buf[CONV_HALO_ROWS - 2 + j:CONV_HALO_ROWS - 2 + j + POST_ROWS, :] for j in range(3)]
        return (w[0:1, :] * taps[0] + w[1:2, :] * taps[1] + w[2:3, :] * taps[2]
                + fcb_ref[:, lo:lo + FFN_CHUNK])

    acc = h
    for c in range(min(FFN_LOOKAHEAD, n_chunks)):
        up_proj(c)
    for c in range(n_chunks):
        if c + FFN_LOOKAHEAD < n_chunks:
            up_proj(c + FFN_LOOKAHEAD)
        act = (jax.nn.silu(conv(c, 0)) * conv(c, 1)).astype(_BF)
        acc = acc + _dot(act, wdn_ref[c * FFN_CHUNK:(c + 1) * FFN_CHUNK, :])
    out_ref[...] = acc


def _post(o, sa, sgb, x, wob, wo, nf, wup, fcw, fcb, wdn):
    s = x.shape[0]
    const = lambda shape: pl.BlockSpec(shape, lambda i: (0,) * len(shape), pipeline_mode=pl.Buffered(1))
    rows = pl.BlockSpec((POST_ROWS, D_MODEL), lambda i: (i, 0))
    return pl.pallas_call(
        _post_kernel,
        grid=(s // POST_ROWS,),
        in_specs=[
            rows, rows, rows, rows,
            const((D_MODEL, D_MODEL)),
            const((D_MODEL, D_MODEL)),
            const((1, D_MODEL)),
            const((D_MODEL, 2 * D_FF)),
            const((3, 2 * D_FF)),
            const((1, 2 * D_FF)),
            const((D_FF, D_MODEL)),
        ],
        out_specs=rows,
        out_shape=jax.ShapeDtypeStruct((s, D_MODEL), _F32),
        scratch_shapes=[
            pltpu.VMEM((CONV_HALO_ROWS, 2 * D_FF), _F32),
            pltpu.VMEM((FFN_LOOKAHEAD + 1, 2, CONV_HALO_ROWS + POST_ROWS, FFN_CHUNK), _F32),
        ],
        compiler_params=pltpu.CompilerParams(
            dimension_semantics=("arbitrary",), vmem_limit_bytes=VMEM_LIMIT),
        name="post",
    )(o, sa, sgb, x, wob, wo, nf, wup, fcw, fcb, wdn)


def kernel(x, norm_mix, w_in, conv_a_w, q_norm, k_norm, w_out_a, w_out_b, w_o, norm_ffn, w_up,
           ffn_conv_w, ffn_conv_b, w_down):
    b, s, d = x.shape
    assert b == 1 and d == D_MODEL and s % POST_ROWS == 0 and s % MOBA_BLOCK == 0
    x2 = x.reshape(s, d)
    wb = w_in.astype(_BF)
    wqt = wb[:, 3 * D_MODEL:4 * D_MODEL].T
    wvt = wb[:, 5 * D_MODEL:6 * D_MODEL].T

    sa, sgb, qt, k, vt, bias, wup, wob, wo, wdn = _proj(
        x2, norm_mix.reshape(1, d), wb, wqt, wvt, conv_a_w, q_norm.reshape(HEAD_DIM, 1),
        k_norm.reshape(1, HEAD_DIM), w_out_a.astype(_BF), w_up, w_out_b, w_o, w_down)
    o = _attn(qt, k, vt, bias)
    out = _post(o, sa, sgb, x2, wob, wo, norm_ffn.reshape(1, d), wup, ffn_conv_w,
                ffn_conv_b.reshape(1, 2 * D_FF), wdn)
    return out.reshape(b, s, d)
```

```python
import functools

import jax
import jax.numpy as jnp
from jax import lax
from jax.experimental import pallas as pl
from jax.experimental.pallas import tpu as pltpu

D_MODEL = 1024
N_HEADS = 8
HEAD_DIM = 128
MOBA_BLOCK = 256
MOBA_TOPK = 3
D_FF = 2816
EPS = 1e-6
CONV_HALO_ROWS = 8
NEG_BIG = -1e30

PROJ_ROWS = 256
WDN_CAST_STEPS = 16
POST_ROWS = 512
FFN_CHUNK = 256
FFN_LOOKAHEAD = 3
VMEM_LIMIT = 56 * 1024 * 1024
ATTN_HEADS = 4
ONES_ROWS = 16
Q_BLOCKS = 2
UNIT_BLOCKS = Q_BLOCKS
UNIT_KEYS = UNIT_BLOCKS * MOBA_BLOCK
Q_TILE = Q_BLOCKS * MOBA_BLOCK
LONG_TRIP_SHIFT = 1
LONG_TRIP_PAIRS = 1 << LONG_TRIP_SHIFT
QK_DEPTH = 2 * HEAD_DIM
QK_SCALE = HEAD_DIM ** -0.5 * 1.4426950408889634

_BF = jnp.bfloat16
_F32 = jnp.float32


def _dot(a, b):
    return jnp.dot(a, b, preferred_element_type=_F32)


def _dot_nt(a, b):
    return lax.dot_general(a, b, (((1,), (1,)), ((), ())), preferred_element_type=_F32)


def _rms(x, axis=-1):
    return x * lax.rsqrt(jnp.mean(x * x, axis=axis, keepdims=True) + EPS)


def _causal_conv3(c, prev, w):
    row = lax.broadcasted_iota(jnp.int32, c.shape, 0)
    p1 = prev[CONV_HALO_ROWS - 1:CONV_HALO_ROWS, :]
    p2 = prev[CONV_HALO_ROWS - 2:CONV_HALO_ROWS - 1, :]
    c1 = jnp.where(row == 0, p1, pltpu.roll(c, 1, 0))
    c2 = jnp.where(row == 0, p2, jnp.where(row == 1, p1, pltpu.roll(c, 2, 0)))
    return w[0:1, :] * c2 + w[1:2, :] * c1 + w[2:3, :] * c


def _proj_kernel(x_ref, nm_ref, wc_ref, wqt_ref, wk_ref, wvt_ref, wg_ref, cw_ref, qn_ref, kn_ref, woa_ref,
                 wup32_ref, wob32_ref, wo32_ref, wdn32_ref,
                 sa_ref, sgb_ref, qt_ref, k_ref, vt_ref, bias_ref, wup_ref, wob_ref, wo_ref, wdn_ref,
                 halo_ref, kbar_ref):
    i = pl.program_id(0)

    @pl.when(i == 0)
    def _():
        halo_ref[...] = jnp.zeros_like(halo_ref)
        kbar_ref[...] = jnp.zeros_like(kbar_ref)

    xb = (_rms(x_ref[...]) * nm_ref[...]).astype(_BF)
    n_blocks = kbar_ref.shape[0]

    zk = _dot(xb, wk_ref[...])
    zqt = _dot_nt(wqt_ref[...], xb)
    zc = _dot(xb, wc_ref[...])

    gates = []
    for h in range(N_HEADS):
        lo = h * HEAD_DIM
        qh = (_rms(zqt[lo:lo + HEAD_DIM, :], axis=0) * qn_ref[...] * QK_SCALE).astype(_BF)
        qt_ref[lo:lo + HEAD_DIM, :] = qh
        gates.append(_dot(kbar_ref[:, lo:lo + HEAD_DIM].astype(_BF), qh))

    zg = _dot(xb, wg_ref[...])

    for h in range(N_HEADS):
        gate = gates[h]
        blk = lax.broadcasted_iota(jnp.int32, gate.shape, 0)
        gate = jnp.where(blk < i, gate, -jnp.inf)
        bias = jnp.full(gate.shape, NEG_BIG, _F32)
        for _ in range(MOBA_TOPK):
            top = jnp.max(gate, axis=0, keepdims=True)
            first = jnp.min(jnp.where(gate == top, blk, n_blocks), axis=0, keepdims=True)
            hit = blk == first
            bias = jnp.where(hit, 0.0, bias)
            gate = jnp.where(hit, -jnp.inf, gate)
        bias_ref[h] = jnp.where(blk < i, bias, NEG_BIG).astype(_BF)

    kbar_rows = []
    for h in range(N_HEADS):
        lo = h * HEAD_DIM
        kh = _rms(zk[:, lo:lo + HEAD_DIM]) * kn_ref[...]
        k_ref[:, lo:lo + HEAD_DIM] = kh.astype(_BF)
        kbar_rows.append(jnp.mean(kh, axis=0, keepdims=True))
    kbar_all = kbar_ref[...]
    blk_row = lax.broadcasted_iota(jnp.int32, kbar_all.shape, 0)
    kbar_ref[...] = jnp.where(blk_row == i, jnp.concatenate(kbar_rows, axis=1), kbar_all)

    bg = zc[:, :D_MODEL]
    c = zc[:, D_MODEL:2 * D_MODEL] * zc[:, 2 * D_MODEL:]
    y = _causal_conv3(c, halo_ref[...], cw_ref[...])
    halo_ref[...] = c[PROJ_ROWS - CONV_HALO_ROWS:, :]
    branch_a = _dot((bg * y).astype(_BF), woa_ref[...])

    vt = _dot_nt(wvt_ref[...], xb)

    sa_ref[...] = jax.nn.sigmoid(zg[:, :D_MODEL]) * branch_a
    sgb_ref[...] = jax.nn.sigmoid(zg[:, D_MODEL:])
    for h in range(N_HEADS):
        vt_ref[h, 0] = vt[h * HEAD_DIM:(h + 1) * HEAD_DIM, :].astype(_BF)

    wup_ref[...] = wup32_ref[...].astype(_BF)
    wob_ref[...] = wob32_ref[...].astype(_BF)
    wo_ref[...] = wo32_ref[...].astype(_BF)

    @pl.when(i < WDN_CAST_STEPS)
    def _():
        wdn_ref[...] = wdn32_ref[...].astype(_BF)


def _proj(x, nm, wb, wqt, wvt, cw, qn, kn, woa, w_up, w_out_b, w_o, w_down):
    s = x.shape[0]
    assert PROJ_ROWS == MOBA_BLOCK
    nb = s // PROJ_ROWS
    cast_rows = D_MODEL // nb
    wdn_rows = D_FF // WDN_CAST_STEPS
    assert cast_rows * nb == D_MODEL and cast_rows % 16 == 0 and wdn_rows % 16 == 0 and nb >= WDN_CAST_STEPS
    const = lambda shape: pl.BlockSpec(shape, lambda i: (0,) * len(shape), pipeline_mode=pl.Buffered(1))
    w_in_cols = lambda width, blk: pl.BlockSpec((D_MODEL, width), lambda i: (0, blk),
                                                pipeline_mode=pl.Buffered(1))
    rows = lambda w: pl.BlockSpec((PROJ_ROWS, w), lambda i: (i, 0))
    slab = lambda w: pl.BlockSpec((cast_rows, w), lambda i: (i, 0))
    wdn_slab = pl.BlockSpec((wdn_rows, D_MODEL), lambda i: (jnp.minimum(i, WDN_CAST_STEPS - 1), 0))
    return pl.pallas_call(
        _proj_kernel,
        grid=(nb,),
        in_specs=[
            rows(D_MODEL),
            const((1, D_MODEL)),
            w_in_cols(3 * D_MODEL, 0),
            const((D_MODEL, D_MODEL)),
            w_in_cols(D_MODEL, 4),
            const((D_MODEL, D_MODEL)),
            w_in_cols(2 * D_MODEL, 3),
            const((3, D_MODEL)),
            const((HEAD_DIM, 1)),
            const((1, HEAD_DIM)),
            const((D_MODEL, D_MODEL)),
            slab(2 * D_FF),
            slab(D_MODEL),
            slab(D_MODEL),
            wdn_slab,
        ],
        out_specs=[
            rows(D_MODEL),
            rows(D_MODEL),
            pl.BlockSpec((D_MODEL, PROJ_ROWS), lambda i: (0, i)),
            rows(D_MODEL),
            pl.BlockSpec((N_HEADS, 1, HEAD_DIM, MOBA_BLOCK), lambda i: (0, i, 0, 0)),
            pl.BlockSpec((N_HEADS, nb, PROJ_ROWS), lambda i: (0, 0, i)),
            slab(2 * D_FF),
            slab(D_MODEL),
            slab(D_MODEL),
            wdn_slab,
        ],
        out_shape=[
            jax.ShapeDtypeStruct((s, D_MODEL), _F32),
            jax.ShapeDtypeStruct((s, D_MODEL), _F32),
            jax.ShapeDtypeStruct((D_MODEL, s), _BF),
            jax.ShapeDtypeStruct((s, D_MODEL), _BF),
            jax.ShapeDtypeStruct((N_HEADS, nb, HEAD_DIM, MOBA_BLOCK), _BF),
            jax.ShapeDtypeStruct((N_HEADS, nb, s), _BF),
            jax.ShapeDtypeStruct((D_MODEL, 2 * D_FF), _BF),
            jax.ShapeDtypeStruct((D_MODEL, D_MODEL), _BF),
            jax.ShapeDtypeStruct((D_MODEL, D_MODEL), _BF),
            jax.ShapeDtypeStruct((D_FF, D_MODEL), _BF),
        ],
        scratch_shapes=[
            pltpu.VMEM((CONV_HALO_ROWS, D_MODEL), _F32),
            pltpu.VMEM((nb, D_MODEL), _F32),
        ],
        compiler_params=pltpu.CompilerParams(
            dimension_semantics=("arbitrary",), vmem_limit_bytes=VMEM_LIMIT),
        name="proj",
    )(x, nm, wb, wqt, wb, wvt, wb, cw, qn, kn, woa, w_up, w_out_b, w_o, w_down)


def _attn_kernel(qt_ref, kblk_ref, vtblk_ref, bias_ref, o_ref, k_ref, vt_ref, qaug_ref, sa_ref, sb_ref,
                 mxa_ref, mxb_ref, m_ref, l_ref, acc_ref, *, n_blocks):
    a = pl.program_id(1)
    own0 = Q_BLOCKS * a
    own1 = own0 + 1
    for b in range(Q_BLOCKS):
        k_ref[own0 + b] = kblk_ref[b]
        for hh in range(ATTN_HEADS):
            vt_ref[hh, own0 + b] = vtblk_ref[hh, b]
    ones_rows = jnp.ones((ONES_ROWS, MOBA_BLOCK), _BF)
    pad_rows = jnp.zeros((QK_DEPTH - HEAD_DIM - n_blocks, Q_TILE), _BF)
    lane = lax.broadcasted_iota(jnp.int32, (ONES_ROWS, HEAD_DIM), 1)

    def head_rows(hh):
        return slice(hh * HEAD_DIM, (hh + 1) * HEAD_DIM)

    def unit_blocks(u):
        js = [UNIT_BLOCKS * (u - 1) + b for b in range(UNIT_BLOCKS)]
        return [jnp.where(j < own0, j, own1) for j in js]

    def keys_with_onehot(j, hh, bias_row):
        row = (lane == bias_row).astype(_BF)
        onehot = jnp.concatenate([row] * (MOBA_BLOCK // ONES_ROWS), axis=0)
        return jnp.concatenate([k_ref[j, :, head_rows(hh)], onehot], axis=1)

    def put_scores(hh, s, dst_ref, mx_ref):
        dst_ref[hh] = s
        mx_ref[hh] = jnp.max(s, axis=0, keepdims=True)

    def scores(u, hh, dst_ref, mx_ref):
        ka = jnp.concatenate([keys_with_onehot(j, hh, j) for j in unit_blocks(u)], axis=0)
        put_scores(hh, _dot(ka, qaug_ref[hh]), dst_ref, mx_ref)

    def first_scores(hh, dst_ref, mx_ref):
        ka = jnp.concatenate([keys_with_onehot(own0, hh, n_blocks), keys_with_onehot(own1, hh, n_blocks)], axis=0)
        s = _dot(ka, qaug_ref[hh])
        kpos = lax.broadcasted_iota(jnp.int32, (MOBA_BLOCK, Q_TILE), 0)
        qcol = lax.broadcasted_iota(jnp.int32, (1, Q_TILE), 1)
        bias = bias_ref[hh].astype(_F32)
        blk = lax.broadcasted_iota(jnp.int32, bias.shape, 0)
        picked = jnp.max(jnp.where(blk == own0, bias, NEG_BIG), axis=0, keepdims=True) >= 0.0
        last0 = jnp.where(qcol < MOBA_BLOCK, qcol, jnp.where(picked, MOBA_BLOCK, -1))
        last1 = qcol - MOBA_BLOCK
        s = jnp.concatenate([jnp.where(kpos <= last0, s[:MOBA_BLOCK], NEG_BIG),
                             jnp.where(kpos <= last1, s[MOBA_BLOCK:], NEG_BIG)], axis=0)
        put_scores(hh, s, dst_ref, mx_ref)

    def update(blocks, hh, src_ref, mx_ref):
        m_old = m_ref[hh]
        m_new = jnp.maximum(m_old, mx_ref[hh])
        alpha = jnp.exp2(m_old - m_new)
        outs = []
        for b, j in enumerate(blocks):
            p = jnp.exp2(src_ref[hh, b * MOBA_BLOCK:(b + 1) * MOBA_BLOCK, :] - m_new).astype(_BF)
            outs.append(_dot(jnp.concatenate([vt_ref[hh, j], ones_rows], axis=0), p))
        out = functools.reduce(jnp.add, outs)
        m_ref[hh] = m_new
        l_ref[hh] = alpha * l_ref[hh] + out[HEAD_DIM:HEAD_DIM + 1]
        acc_ref[hh] = alpha * acc_ref[hh] + out[:HEAD_DIM]

    for hh in range(ATTN_HEADS):
        qaug_ref[hh] = jnp.concatenate([qt_ref[head_rows(hh), :], bias_ref[hh], pad_rows], axis=0)
        m_ref[hh] = jnp.full(m_ref.shape[1:], NEG_BIG, _F32)
        l_ref[hh] = jnp.zeros(l_ref.shape[1:], _F32)
        acc_ref[hh] = jnp.zeros(acc_ref.shape[1:], _F32)
    for hh in range(ATTN_HEADS):
        first_scores(hh, sa_ref, mxa_ref)

    def unit_pairs(u0, n_pairs, first=False):
        for pair in range(n_pairs):
            u = u0 + 2 * pair
            for hh in range(ATTN_HEADS):
                scores(u + 1, hh, sb_ref, mxb_ref)
                update([own0, own1] if first and pair == 0 else unit_blocks(u), hh, sa_ref, mxa_ref)
            for hh in range(ATTN_HEADS):
                scores(u + 2, hh, sa_ref, mxa_ref)
                update(unit_blocks(u + 1), hh, sb_ref, mxb_ref)

    unit_pairs(0, 1, first=True)
    left = jnp.maximum(a - 1, 0)
    more_pairs = lax.shift_right_logical(left, 1)
    long_trips = lax.shift_right_logical(more_pairs, LONG_TRIP_SHIFT)

    def long_body(t, carry):
        unit_pairs(2 + 2 * LONG_TRIP_PAIRS * t, LONG_TRIP_PAIRS)
        return carry

    def short_body(t, carry):
        unit_pairs(2 + 2 * (LONG_TRIP_PAIRS * long_trips + t), 1)
        return carry

    def last_unit_body(t, carry):
        for hh in range(ATTN_HEADS):
            update(unit_blocks(2 + 2 * more_pairs), hh, sa_ref, mxa_ref)
        return carry

    lax.fori_loop(0, long_trips, long_body, 0)
    lax.fori_loop(0, more_pairs - LONG_TRIP_PAIRS * long_trips, short_body, 0)
    lax.fori_loop(0, left - 2 * more_pairs, last_unit_body, 0)
    for hh in range(ATTN_HEADS):
        o_ref[:, head_rows(hh)] = (acc_ref[hh] * (1.0 / l_ref[hh])).astype(o_ref.dtype).T


def _attn(qt, k, vt, bias):
    s = k.shape[0]
    nb = s // MOBA_BLOCK
    assert nb < QK_DEPTH - HEAD_DIM and nb % Q_BLOCKS == 0 and Q_BLOCKS == 2
    k3 = k.reshape(nb, MOBA_BLOCK, D_MODEL)
    gw = ATTN_HEADS * HEAD_DIM
    scores_buf = pltpu.VMEM((ATTN_HEADS, UNIT_KEYS, Q_TILE), _F32)
    row_buf = pltpu.VMEM((ATTN_HEADS, 1, Q_TILE), _F32)
    return pl.pallas_call(
        functools.partial(_attn_kernel, n_blocks=nb),
        grid=(N_HEADS // ATTN_HEADS, nb // Q_BLOCKS),
        in_specs=[
            pl.BlockSpec((gw, Q_TILE), lambda g, a: (g, a)),
            pl.BlockSpec((Q_BLOCKS, MOBA_BLOCK, gw), lambda g, a: (a, 0, g)),
            pl.BlockSpec((ATTN_HEADS, Q_BLOCKS, HEAD_DIM, MOBA_BLOCK), lambda g, a: (g, a, 0, 0)),
            pl.BlockSpec((ATTN_HEADS, nb, Q_TILE), lambda g, a: (g, 0, a)),
        ],
        out_specs=pl.BlockSpec((Q_TILE, gw), lambda g, a: (a, g)),
        out_shape=jax.ShapeDtypeStruct((s, D_MODEL), _BF),
        scratch_shapes=[
            pltpu.VMEM((nb, MOBA_BLOCK, gw), _BF),
            pltpu.VMEM((ATTN_HEADS, nb, HEAD_DIM, MOBA_BLOCK), _BF),
            pltpu.VMEM((ATTN_HEADS, QK_DEPTH, Q_TILE), _BF),
            scores_buf,
            scores_buf,
            row_buf,
            row_buf,
            row_buf,
            row_buf,
            pltpu.VMEM((ATTN_HEADS, HEAD_DIM, Q_TILE), _F32),
        ],
        compiler_params=pltpu.CompilerParams(
            dimension_semantics=("arbitrary", "arbitrary"), vmem_limit_bytes=VMEM_LIMIT),
        name="moba_attn",
    )(qt, k3, vt, bias)


def _post_kernel(o_ref, sa_ref, sgb_ref, x_ref, wob_ref, wo_ref, nf_ref, wup_ref, fcw_ref, fcb_ref,
                 wdn_ref, out_ref, halo_ref, ubuf_ref):
    i = pl.program_id(0)

    @pl.when(i == 0)
    def _():
        halo_ref[...] = jnp.zeros_like(halo_ref)

    branch_b = _dot(o_ref[...], wob_ref[...])
    merged = sa_ref[...] + sgb_ref[...] * branch_b
    h = x_ref[...] + _dot(merged.astype(_BF), wo_ref[...])
    hn = (_rms(h) * nf_ref[...]).astype(_BF)

    n_chunks = D_FF // FFN_CHUNK

    def up_proj(c):
        for part in range(2):
            lo = part * D_FF + c * FFN_CHUNK
            u = _dot(hn, wup_ref[:, lo:lo + FFN_CHUNK])
            buf = ubuf_ref.at[c % (FFN_LOOKAHEAD + 1), part]
            buf[:CONV_HALO_ROWS, :] = halo_ref[:, lo:lo + FFN_CHUNK]
            buf[CONV_HALO_ROWS:, :] = u
            halo_ref[:, lo:lo + FFN_CHUNK] = u[POST_ROWS - CONV_HALO_ROWS:, :]

    def conv(c, part):
        lo = part * D_FF + c * FFN_CHUNK
        buf = ubuf_ref.at[c % (FFN_LOOKAHEAD + 1), part]
        w = fcw_ref[:, lo:lo + FFN_CHUNK]
        taps = [buf[CONV_HALO_ROWS - 2 + j:CONV_HALO_ROWS - 2 + j + POST_ROWS, :] for j in range(3)]
        return (w[0:1, :] * taps[0] + w[1:2, :] * taps[1] + w[2:3, :] * taps[2]
                + fcb_ref[:, lo:lo + FFN_CHUNK])

    acc = h
    for c in range(min(FFN_LOOKAHEAD, n_chunks)):
        up_proj(c)
    for c in range(n_chunks):
        if c + FFN_LOOKAHEAD < n_chunks:
            up_proj(c + FFN_LOOKAHEAD)
        act = (jax.nn.silu(conv(c, 0)) * conv(c, 1)).astype(_BF)
        acc = acc + _dot(act, wdn_ref[c * FFN_CHUNK:(c + 1) * FFN_CHUNK, :])
    out_ref[...] = acc


def _post(o, sa, sgb, x, wob, wo, nf, wup, fcw, fcb, wdn):
    s = x.shape[0]
    const = lambda shape: pl.BlockSpec(shape, lambda i: (0,) * len(shape), pipeline_mode=pl.Buffered(1))
    rows = pl.BlockSpec((POST_ROWS, D_MODEL), lambda i: (i, 0))
    return pl.pallas_call(
        _post_kernel,
        grid=(s // POST_ROWS,),
        in_specs=[
            rows, rows, rows, rows,
            const((D_MODEL, D_MODEL)),
            const((D_MODEL, D_MODEL)),
            const((1, D_MODEL)),
            const((D_MODEL, 2 * D_FF)),
            const((3, 2 * D_FF)),
            const((1, 2 * D_FF)),
            const((D_FF, D_MODEL)),
        ],
        out_specs=rows,
        out_shape=jax.ShapeDtypeStruct((s, D_MODEL), _F32),
        scratch_shapes=[
            pltpu.VMEM((CONV_HALO_ROWS, 2 * D_FF), _F32),
            pltpu.VMEM((FFN_LOOKAHEAD + 1, 2, CONV_HALO_ROWS + POST_ROWS, FFN_CHUNK), _F32),
        ],
        compiler_params=pltpu.CompilerParams(
            dimension_semantics=("arbitrary",), vmem_limit_bytes=VMEM_LIMIT),
        name="post",
    )(o, sa, sgb, x, wob, wo, nf, wup, fcw, fcb, wdn)


def kernel(x, norm_mix, w_in, conv_a_w, q_norm, k_norm, w_out_a, w_out_b, w_o, norm_ffn, w_up,
           ffn_conv_w, ffn_conv_b, w_down):
    b, s, d = x.shape
    assert b == 1 and d == D_MODEL and s % POST_ROWS == 0 and s % MOBA_BLOCK == 0
    x2 = x.reshape(s, d)
    wb = w_in.astype(_BF)
    wqt = wb[:, 3 * D_MODEL:4 * D_MODEL].T
    wvt = wb[:, 5 * D_MODEL:6 * D_MODEL].T

    sa, sgb, qt, k, vt, bias, wup, wob, wo, wdn = _proj(
        x2, norm_mix.reshape(1, d), wb, wqt, wvt, conv_a_w, q_norm.reshape(HEAD_DIM, 1),
        k_norm.reshape(1, HEAD_DIM), w_out_a.astype(_BF), w_up, w_out_b, w_o, w_down)
    o = _attn(qt, k, vt, bias)
    out = _post(o, sa, sgb, x2, wob, wo, norm_ffn.reshape(1, d), wup, ffn_conv_w,
                ffn_conv_b.reshape(1, 2 * D_FF), wdn)
    return out.reshape(b, s, d)
```

```python
import functools

import jax
import jax.numpy as jnp
from jax import lax
from jax.experimental import pallas as pl
from jax.experimental.pallas import tpu as pltpu

D_MODEL = 1024
N_HEADS = 8
HEAD_DIM = 128
MOBA_BLOCK = 256
MOBA_TOPK = 3
D_FF = 2816
EPS = 1e-6
CONV_HALO_ROWS = 8
NEG_BIG = -1e30

PROJ_ROWS = 256
WDN_CAST_STEPS = 16
POST_ROWS = 512
FFN_CHUNK = 256
FFN_LOOKAHEAD = 3
VMEM_LIMIT = 56 * 1024 * 1024
ATTN_HEADS = 4
ONES_ROWS = 16
Q_BLOCKS = 2
UNIT_BLOCKS = Q_BLOCKS
UNIT_KEYS = UNIT_BLOCKS * MOBA_BLOCK
Q_TILE = Q_BLOCKS * MOBA_BLOCK
LONG_TRIP_SHIFT = 2
LONG_TRIP_PAIRS = 1 << LONG_TRIP_SHIFT
QK_DEPTH = 2 * HEAD_DIM
QK_SCALE = HEAD_DIM ** -0.5 * 1.4426950408889634

_BF = jnp.bfloat16
_F32 = jnp.float32


def _dot(a, b):
    return jnp.dot(a, b, preferred_element_type=_F32)


def _dot_nt(a, b):
    return lax.dot_general(a, b, (((1,), (1,)), ((), ())), preferred_element_type=_F32)


def _rms(x, axis=-1):
    return x * lax.rsqrt(jnp.mean(x * x, axis=axis, keepdims=True) + EPS)


def _causal_conv3(c, prev, w):
    row = lax.broadcasted_iota(jnp.int32, c.shape, 0)
    p1 = prev[CONV_HALO_ROWS - 1:CONV_HALO_ROWS, :]
    p2 = prev[CONV_HALO_ROWS - 2:CONV_HALO_ROWS - 1, :]
    c1 = jnp.where(row == 0, p1, pltpu.roll(c, 1, 0))
    c2 = jnp.where(row == 0, p2, jnp.where(row == 1, p1, pltpu.roll(c, 2, 0)))
    return w[0:1, :] * c2 + w[1:2, :] * c1 + w[2:3, :] * c


def _proj_kernel(x_ref, nm_ref, wc_ref, wqt_ref, wk_ref, wvt_ref, wg_ref, cw_ref, qn_ref, kn_ref, woa_ref,
                 wup32_ref, wob32_ref, wo32_ref, wdn32_ref,
                 sa_ref, sgb_ref, qt_ref, k_ref, vt_ref, bias_ref, wup_ref, wob_ref, wo_ref, wdn_ref,
                 halo_ref, kbar_ref):
    i = pl.program_id(0)

    @pl.when(i == 0)
    def _():
        halo_ref[...] = jnp.zeros_like(halo_ref)
        kbar_ref[...] = jnp.zeros_like(kbar_ref)

    xb = (_rms(x_ref[...]) * nm_ref[...]).astype(_BF)
    n_blocks = kbar_ref.shape[0]

    zk = _dot(xb, wk_ref[...])
    zqt = _dot_nt(wqt_ref[...], xb)
    zc = _dot(xb, wc_ref[...])

    gates = []
    for h in range(N_HEADS):
        lo = h * HEAD_DIM
        qh = (_rms(zqt[lo:lo + HEAD_DIM, :], axis=0) * qn_ref[...] * QK_SCALE).astype(_BF)
        qt_ref[lo:lo + HEAD_DIM, :] = qh
        gates.append(_dot(kbar_ref[:, lo:lo + HEAD_DIM].astype(_BF), qh))

    zg = _dot(xb, wg_ref[...])

    for h in range(N_HEADS):
        gate = gates[h]
        blk = lax.broadcasted_iota(jnp.int32, gate.shape, 0)
        gate = jnp.where(blk < i, gate, -jnp.inf)
        bias = jnp.full(gate.shape, NEG_BIG, _F32)
        for _ in range(MOBA_TOPK):
            top = jnp.max(gate, axis=0, keepdims=True)
            first = jnp.min(jnp.where(gate == top, blk, n_blocks), axis=0, keepdims=True)
            hit = blk == first
            bias = jnp.where(hit, 0.0, bias)
            gate = jnp.where(hit, -jnp.inf, gate)
        bias_ref[h] = jnp.where(blk < i, bias, NEG_BIG).astype(_BF)

    kbar_rows = []
    for h in range(N_HEADS):
        lo = h * HEAD_DIM
        kh = _rms(zk[:, lo:lo + HEAD_DIM]) * kn_ref[...]
        k_ref[:, lo:lo + HEAD_DIM] = kh.astype(_BF)
        kbar_rows.append(jnp.mean(kh, axis=0, keepdims=True))
    kbar_all = kbar_ref[...]
    blk_row = lax.broadcasted_iota(jnp.int32, kbar_all.shape, 0)
    kbar_ref[...] = jnp.where(blk_row == i, jnp.concatenate(kbar_rows, axis=1), kbar_all)

    bg = zc[:, :D_MODEL]
    c = zc[:, D_MODEL:2 * D_MODEL] * zc[:, 2 * D_MODEL:]
    y = _causal_conv3(c, halo_ref[...], cw_ref[...])
    halo_ref[...] = c[PROJ_ROWS - CONV_HALO_ROWS:, :]
    branch_a = _dot((bg * y).astype(_BF), woa_ref[...])

    vt = _dot_nt(wvt_ref[...], xb)

    sa_ref[...] = jax.nn.sigmoid(zg[:, :D_MODEL]) * branch_a
    sgb_ref[...] = jax.nn.sigmoid(zg[:, D_MODEL:])
    for h in range(N_HEADS):
        vt_ref[h, 0] = vt[h * HEAD_DIM:(h + 1) * HEAD_DIM, :].astype(_BF)

    wup_ref[...] = wup32_ref[...].astype(_BF)
    wob_ref[...] = wob32_ref[...].astype(_BF)
    wo_ref[...] = wo32_ref[...].astype(_BF)

    @pl.when(i < WDN_CAST_STEPS)
    def _():
        wdn_ref[...] = wdn32_ref[...].astype(_BF)


def _proj(x, nm, wb, wqt, wvt, cw, qn, kn, woa, w_up, w_out_b, w_o, w_down):
    s = x.shape[0]
    assert PROJ_ROWS == MOBA_BLOCK
    nb = s // PROJ_ROWS
    cast_rows = D_MODEL // nb
    wdn_rows = D_FF // WDN_CAST_STEPS
    assert cast_rows * nb == D_MODEL and cast_rows % 16 == 0 and wdn_rows % 16 == 0 and nb >= WDN_CAST_STEPS
    const = lambda shape: pl.BlockSpec(shape, lambda i: (0,) * len(shape), pipeline_mode=pl.Buffered(1))
    w_in_cols = lambda width, blk: pl.BlockSpec((D_MODEL, width), lambda i: (0, blk),
                                                pipeline_mode=pl.Buffered(1))
    rows = lambda w: pl.BlockSpec((PROJ_ROWS, w), lambda i: (i, 0))
    slab = lambda w: pl.BlockSpec((cast_rows, w), lambda i: (i, 0))
    wdn_slab = pl.BlockSpec((wdn_rows, D_MODEL), lambda i: (jnp.minimum(i, WDN_CAST_STEPS - 1), 0))
    return pl.pallas_call(
        _proj_kernel,
        grid=(nb,),
        in_specs=[
            rows(D_MODEL),
            const((1, D_MODEL)),
            w_in_cols(3 * D_MODEL, 0),
            const((D_MODEL, D_MODEL)),
            w_in_cols(D_MODEL, 4),
            const((D_MODEL, D_MODEL)),
            w_in_cols(2 * D_MODEL, 3),
            const((3, D_MODEL)),
            const((HEAD_DIM, 1)),
            const((1, HEAD_DIM)),
            const((D_MODEL, D_MODEL)),
            slab(2 * D_FF),
            slab(D_MODEL),
            slab(D_MODEL),
            wdn_slab,
        ],
        out_specs=[
            rows(D_MODEL),
            rows(D_MODEL),
            pl.BlockSpec((D_MODEL, PROJ_ROWS), lambda i: (0, i)),
            rows(D_MODEL),
            pl.BlockSpec((N_HEADS, 1, HEAD_DIM, MOBA_BLOCK), lambda i: (0, i, 0, 0)),
            pl.BlockSpec((N_HEADS, nb, PROJ_ROWS), lambda i: (0, 0, i)),
            slab(2 * D_FF),
            slab(D_MODEL),
            slab(D_MODEL),
            wdn_slab,
        ],
        out_shape=[
            jax.ShapeDtypeStruct((s, D_MODEL), _F32),
            jax.ShapeDtypeStruct((s, D_MODEL), _F32),
            jax.ShapeDtypeStruct((D_MODEL, s), _BF),
            jax.ShapeDtypeStruct((s, D_MODEL), _BF),
            jax.ShapeDtypeStruct((N_HEADS, nb, HEAD_DIM, MOBA_BLOCK), _BF),
            jax.ShapeDtypeStruct((N_HEADS, nb, s), _BF),
            jax.ShapeDtypeStruct((D_MODEL, 2 * D_FF), _BF),
            jax.ShapeDtypeStruct((D_MODEL, D_MODEL), _BF),
            jax.ShapeDtypeStruct((D_MODEL, D_MODEL), _BF),
            jax.ShapeDtypeStruct((D_FF, D_MODEL), _BF),
        ],
        scratch_shapes=[
            pltpu.VMEM((CONV_HALO_ROWS, D_MODEL), _F32),
            pltpu.VMEM((nb, D_MODEL), _F32),
        ],
        compiler_params=pltpu.CompilerParams(
            dimension_semantics=("arbitrary",), vmem_limit_bytes=VMEM_LIMIT),
        name="proj",
    )(x, nm, wb, wqt, wb, wvt, wb, cw, qn, kn, woa, w_up, w_out_b, w_o, w_down)


def _attn_kernel(qt_ref, kblk_ref, vtblk_ref, bias_ref, o_ref, k_ref, vt_ref, qaug_ref, sa_ref, sb_ref,
                 mxa_ref, mxb_ref, m_ref, l_ref, acc_ref, *, n_blocks):
    a = pl.program_id(1)
    own0 = Q_BLOCKS * a
    own1 = own0 + 1
    for b in range(Q_BLOCKS):
        k_ref[own0 + b] = kblk_ref[b]
        for hh in range(ATTN_HEADS):
            vt_ref[hh, own0 + b] = vtblk_ref[hh, b]
    ones_rows = jnp.ones((ONES_ROWS, MOBA_BLOCK), _BF)
    pad_rows = jnp.zeros((QK_DEPTH - HEAD_DIM - n_blocks, Q_TILE), _BF)
    lane = lax.broadcasted_iota(jnp.int32, (ONES_ROWS, HEAD_DIM), 1)

    def head_rows(hh):
        return slice(hh * HEAD_DIM, (hh + 1) * HEAD_DIM)

    def unit_blocks(u):
        js = [UNIT_BLOCKS * (u - 1) + b for b in range(UNIT_BLOCKS)]
        return [jnp.where(j < own0, j, own1) for j in js]

    def keys_with_onehot(j, hh, bias_row):
        row = (lane == bias_row).astype(_BF)
        onehot = jnp.concatenate([row] * (MOBA_BLOCK // ONES_ROWS), axis=0)
        return jnp.concatenate([k_ref[j, :, head_rows(hh)], onehot], axis=1)

    def put_scores(hh, s, dst_ref, mx_ref):
        dst_ref[hh] = s
        mx_ref[hh] = jnp.max(s, axis=0, keepdims=True)

    def scores(u, hh, dst_ref, mx_ref):
        ka = jnp.concatenate([keys_with_onehot(j, hh, j) for j in unit_blocks(u)], axis=0)
        put_scores(hh, _dot(ka, qaug_ref[hh]), dst_ref, mx_ref)

    def first_scores(hh, dst_ref, mx_ref):
        ka = jnp.concatenate([keys_with_onehot(own0, hh, n_blocks), keys_with_onehot(own1, hh, n_blocks)], axis=0)
        s = _dot(ka, qaug_ref[hh])
        kpos = lax.broadcasted_iota(jnp.int32, (MOBA_BLOCK, Q_TILE), 0)
        qcol = lax.broadcasted_iota(jnp.int32, (1, Q_TILE), 1)
        bias = bias_ref[hh].astype(_F32)
        blk = lax.broadcasted_iota(jnp.int32, bias.shape, 0)
        picked = jnp.max(jnp.where(blk == own0, bias, NEG_BIG), axis=0, keepdims=True) >= 0.0
        last0 = jnp.where(qcol < MOBA_BLOCK, qcol, jnp.where(picked, MOBA_BLOCK, -1))
        last1 = qcol - MOBA_BLOCK
        s = jnp.concatenate([jnp.where(kpos <= last0, s[:MOBA_BLOCK], NEG_BIG),
                             jnp.where(kpos <= last1, s[MOBA_BLOCK:], NEG_BIG)], axis=0)
        put_scores(hh, s, dst_ref, mx_ref)

    def update(blocks, hh, src_ref, mx_ref):
        m_old = m_ref[hh]
        m_new = jnp.maximum(m_old, mx_ref[hh])
        alpha = jnp.exp2(m_old - m_new)
        outs = []
        for b, j in enumerate(blocks):
            p = jnp.exp2(src_ref[hh, b * MOBA_BLOCK:(b + 1) * MOBA_BLOCK, :] - m_new).astype(_BF)
            outs.append(_dot(jnp.concatenate([vt_ref[hh, j], ones_rows], axis=0), p))
        out = functools.reduce(jnp.add, outs)
        m_ref[hh] = m_new
        l_ref[hh] = alpha * l_ref[hh] + out[HEAD_DIM:HEAD_DIM + 1]
        acc_ref[hh] = alpha * acc_ref[hh] + out[:HEAD_DIM]

    for hh in range(ATTN_HEADS):
        qaug_ref[hh] = jnp.concatenate([qt_ref[head_rows(hh), :], bias_ref[hh], pad_rows], axis=0)
        m_ref[hh] = jnp.full(m_ref.shape[1:], NEG_BIG, _F32)
        l_ref[hh] = jnp.zeros(l_ref.shape[1:], _F32)
        acc_ref[hh] = jnp.zeros(acc_ref.shape[1:], _F32)
    for hh in range(ATTN_HEADS):
        first_scores(hh, sa_ref, mxa_ref)

    def unit_pairs(u0, n_pairs, first=False):
        for pair in range(n_pairs):
            u = u0 + 2 * pair
            for hh in range(ATTN_HEADS):
                scores(u + 1, hh, sb_ref, mxb_ref)
                update([own0, own1] if first and pair == 0 else unit_blocks(u), hh, sa_ref, mxa_ref)
            for hh in range(ATTN_HEADS):
                scores(u + 2, hh, sa_ref, mxa_ref)
                update(unit_blocks(u + 1), hh, sb_ref, mxb_ref)

    unit_pairs(0, 1, first=True)
    left = jnp.maximum(a - 1, 0)
    more_pairs = lax.shift_right_logical(left, 1)
    long_trips = lax.shift_right_logical(more_pairs, LONG_TRIP_SHIFT)

    def long_body(t, carry):
        unit_pairs(2 + 2 * LONG_TRIP_PAIRS * t, LONG_TRIP_PAIRS)
        return carry

    def short_body(t, carry):
        unit_pairs(2 + 2 * (LONG_TRIP_PAIRS * long_trips + t), 1)
        return carry

    def last_unit_body(t, carry):
        for hh in range(ATTN_HEADS):
            update(unit_blocks(2 + 2 * more_pairs), hh, sa_ref, mxa_ref)
        return carry

    lax.fori_loop(0, long_trips, long_body, 0)
    lax.fori_loop(0, more_pairs - LONG_TRIP_PAIRS * long_trips, short_body, 0)
    lax.fori_loop(0, left - 2 * more_pairs, last_unit_body, 0)
    for hh in range(ATTN_HEADS):
        o_ref[:, head_rows(hh)] = (acc_ref[hh] * (1.0 / l_ref[hh])).astype(o_ref.dtype).T


def _attn(qt, k, vt, bias):
    s = k.shape[0]
    nb = s // MOBA_BLOCK
    assert nb < QK_DEPTH - HEAD_DIM and nb % Q_BLOCKS == 0 and Q_BLOCKS == 2
    k3 = k.reshape(nb, MOBA_BLOCK, D_MODEL)
    gw = ATTN_HEADS * HEAD_DIM
    scores_buf = pltpu.VMEM((ATTN_HEADS, UNIT_KEYS, Q_TILE), _F32)
    row_buf = pltpu.VMEM((ATTN_HEADS, 1, Q_TILE), _F32)
    return pl.pallas_call(
        functools.partial(_attn_kernel, n_blocks=nb),
        grid=(N_HEADS // ATTN_HEADS, nb // Q_BLOCKS),
        in_specs=[
            pl.BlockSpec((gw, Q_TILE), lambda g, a: (g, a)),
            pl.BlockSpec((Q_BLOCKS, MOBA_BLOCK, gw), lambda g, a: (a, 0, g)),
            pl.BlockSpec((ATTN_HEADS, Q_BLOCKS, HEAD_DIM, MOBA_BLOCK), lambda g, a: (g, a, 0, 0)),
            pl.BlockSpec((ATTN_HEADS, nb, Q_TILE), lambda g, a: (g, 0, a)),
        ],
        out_specs=pl.BlockSpec((Q_TILE, gw), lambda g, a: (a, g)),
        out_shape=jax.ShapeDtypeStruct((s, D_MODEL), _BF),
        scratch_shapes=[
            pltpu.VMEM((nb, MOBA_BLOCK, gw), _BF),
            pltpu.VMEM((ATTN_HEADS, nb, HEAD_DIM, MOBA_BLOCK), _BF),
            pltpu.VMEM((ATTN_HEADS, QK_DEPTH, Q_TILE), _BF),
            scores_buf,
            scores_buf,
            row_buf,
            row_buf,
            row_buf,
            row_buf,
            pltpu.VMEM((ATTN_HEADS, HEAD_DIM, Q_TILE), _F32),
        ],
        compiler_params=pltpu.CompilerParams(
            dimension_semantics=("arbitrary", "arbitrary"), vmem_limit_bytes=VMEM_LIMIT),
        name="moba_attn",
    )(qt, k3, vt, bias)


def _post_kernel(o_ref, sa_ref, sgb_ref, x_ref, wob_ref, wo_ref, nf_ref, wup_ref, fcw_ref, fcb_ref,
                 wdn_ref, out_ref, halo_ref, ubuf_ref):
    i = pl.program_id(0)

    @pl.when(i == 0)
    def _():
        halo_ref[...] = jnp.zeros_like(halo_ref)

    branch_b = _dot(o_ref[...], wob_ref[...])
    merged = sa_ref[...] + sgb_ref[...] * branch_b
    h = x_ref[...] + _dot(merged.astype(_BF), wo_ref[...])
    hn = (_rms(h) * nf_ref[...]).astype(_BF)

    n_chunks = D_FF // FFN_CHUNK

    def up_proj(c):
        for part in range(2):
            lo = part * D_FF + c * FFN_CHUNK
            u = _dot(hn, wup_ref[:, lo:lo + FFN_CHUNK])
            buf = ubuf_ref.at[c % (FFN_LOOKAHEAD + 1), part]
            buf[:CONV_HALO_ROWS, :] = halo_ref[:, lo:lo + FFN_CHUNK]
            buf[CONV_HALO_ROWS:, :] = u
            halo_ref[:, lo:lo + FFN_CHUNK] = u[POST_ROWS - CONV_HALO_ROWS:, :]

    def conv(c, part):
        lo = part * D_FF + c * FFN_CHUNK
        buf = ubuf_ref.at[c % (FFN_LOOKAHEAD + 1), part]
        w = fcw_ref[:, lo:lo + FFN_CHUNK]
        taps = [buf[CONV_HALO_ROWS - 2 + j:CONV_HALO_ROWS - 2 + j + POST_ROWS, :] for j in range(3)]
        return (w[0:1, :] * taps[0] + w[1:2, :] * taps[1] + w[2:3, :] * taps[2]
                + fcb_ref[:, lo:lo + FFN_CHUNK])

    acc = h
    for c in range(min(FFN_LOOKAHEAD, n_chunks)):
        up_proj(c)
    for c in range(n_chunks):
        if c + FFN_LOOKAHEAD < n_chunks:
            up_proj(c + FFN_LOOKAHEAD)
        half_g = 0.5 * conv(c, 0)
        act = ((half_g + half_g * jnp.tanh(half_g)) * conv(c, 1)).astype(_BF)
        acc = acc + _dot(act, wdn_ref[c * FFN_CHUNK:(c + 1) * FFN_CHUNK, :])
    out_ref[...] = acc


def _post(o, sa, sgb, x, wob, wo, nf, wup, fcw, fcb, wdn):
    s = x.shape[0]
    const = lambda shape: pl.BlockSpec(shape, lambda i: (0,) * len(shape), pipeline_mode=pl.Buffered(1))
    rows = pl.BlockSpec((POST_ROWS, D_MODEL), lambda i: (i, 0))
    return pl.pallas_call(
        _post_kernel,
        grid=(s // POST_ROWS,),
        in_specs=[
            rows, rows, rows, rows,
            const((D_MODEL, D_MODEL)),
            const((D_MODEL, D_MODEL)),
            const((1, D_MODEL)),
            const((D_MODEL, 2 * D_FF)),
            const((3, 2 * D_FF)),
            const((1, 2 * D_FF)),
            const((D_FF, D_MODEL)),
        ],
        out_specs=rows,
        out_shape=jax.ShapeDtypeStruct((s, D_MODEL), _F32),
        scratch_shapes=[
            pltpu.VMEM((CONV_HALO_ROWS, 2 * D_FF), _F32),
            pltpu.VMEM((FFN_LOOKAHEAD + 1, 2, CONV_HALO_ROWS + POST_ROWS, FFN_CHUNK), _F32),
        ],
        compiler_params=pltpu.CompilerParams(
            dimension_semantics=("arbitrary",), vmem_limit_bytes=VMEM_LIMIT),
        name="post",
    )(o, sa, sgb, x, wob, wo, nf, wup, fcw, fcb, wdn)


def kernel(x, norm_mix, w_in, conv_a_w, q_norm, k_norm, w_out_a, w_out_b, w_o, norm_ffn, w_up,
           ffn_conv_w, ffn_conv_b, w_down):
    b, s, d = x.shape
    assert b == 1 and d == D_MODEL and s % POST_ROWS == 0 and s % MOBA_BLOCK == 0
    x2 = x.reshape(s, d)
    wb = w_in.astype(_BF)
    wqt = wb[:, 3 * D_MODEL:4 * D_MODEL].T
    wvt = wb[:, 5 * D_MODEL:6 * D_MODEL].T

    sa, sgb, qt, k, vt, bias, wup, wob, wo, wdn = _proj(
        x2, norm_mix.reshape(1, d), wb, wqt, wvt, conv_a_w, q_norm.reshape(HEAD_DIM, 1),
        k_norm.reshape(1, HEAD_DIM), w_out_a.astype(_BF), w_up, w_out_b, w_o, w_down)
    o = _attn(qt, k, vt, bias)
    out = _post(o, sa, sgb, x2, wob, wo, norm_ffn.reshape(1, d), wup, ffn_conv_w,
                ffn_conv_b.reshape(1, 2 * D_FF), wdn)
    return out.reshape(b, s, d)
```

```python
import functools

import jax
import jax.numpy as jnp
from jax import lax
from jax.experimental import pallas as pl
from jax.experimental.pallas import tpu as pltpu

D_MODEL = 1024
N_HEADS = 8
HEAD_DIM = 128
MOBA_BLOCK = 256
MOBA_TOPK = 3
D_FF = 2816
EPS = 1e-6
CONV_HALO_ROWS = 8
NEG_BIG = -1e30

W_IN_Q_GROUP = 3
W_IN_V_GROUP = 5
PROJ_ROWS = 256
WDN_CAST_STEPS = 16
POST_ROWS = 512
FFN_CHUNK = 256
FFN_LOOKAHEAD = 3
VMEM_LIMIT = 56 * 1024 * 1024
ATTN_HEADS = 4
ONES_ROWS = 16
Q_BLOCKS = 2
UNIT_BLOCKS = Q_BLOCKS
UNIT_KEYS = UNIT_BLOCKS * MOBA_BLOCK
Q_TILE = Q_BLOCKS * MOBA_BLOCK
LONG_TRIP_SHIFT = 2
LONG_TRIP_PAIRS = 1 << LONG_TRIP_SHIFT
QK_DEPTH = 2 * HEAD_DIM
QK_SCALE = HEAD_DIM ** -0.5 * 1.4426950408889634

_BF = jnp.bfloat16
_F32 = jnp.float32


def _dot(a, b):
    return jnp.dot(a, b, preferred_element_type=_F32)


def _dot_nt(a, b):
    return lax.dot_general(a, b, (((1,), (1,)), ((), ())), preferred_element_type=_F32)


def _rms(x, axis=-1):
    return x * lax.rsqrt(jnp.mean(x * x, axis=axis, keepdims=True) + EPS)


def _causal_conv3(c, prev, w):
    row = lax.broadcasted_iota(jnp.int32, c.shape, 0)
    p1 = prev[CONV_HALO_ROWS - 1:CONV_HALO_ROWS, :]
    p2 = prev[CONV_HALO_ROWS - 2:CONV_HALO_ROWS - 1, :]
    c1 = jnp.where(row == 0, p1, pltpu.roll(c, 1, 0))
    c2 = jnp.where(row == 0, p2, jnp.where(row == 1, p1, pltpu.roll(c, 2, 0)))
    return w[0:1, :] * c2 + w[1:2, :] * c1 + w[2:3, :] * c


def _prep_kernel(w_ref, woa32_ref, wb_ref, wqt_ref, wvt_ref, woa_ref):
    c = pl.program_id(0)
    wb_ref[...] = w_ref[...].astype(_BF)
    woa_ref[...] = woa32_ref[...].astype(_BF)

    @pl.when(c == W_IN_Q_GROUP)
    def _():
        wqt_ref[...] = w_ref[...].T.astype(_BF)

    @pl.when(c == W_IN_V_GROUP)
    def _():
        wvt_ref[...] = w_ref[...].T.astype(_BF)


def _prep(w_in, w_out_a):
    groups = w_in.shape[1] // D_MODEL
    slab_rows = D_MODEL // groups
    assert slab_rows * groups == D_MODEL and slab_rows % 16 == 0
    square = lambda index_map: pl.BlockSpec((D_MODEL, D_MODEL), index_map)
    slab = pl.BlockSpec((slab_rows, D_MODEL), lambda c: (c, 0))
    return pl.pallas_call(
        _prep_kernel,
        grid=(groups,),
        in_specs=[square(lambda c: (0, c)), slab],
        out_specs=[square(lambda c: (0, c)), square(lambda c: (0, 0)), square(lambda c: (0, 0)), slab],
        out_shape=[
            jax.ShapeDtypeStruct(w_in.shape, _BF),
            jax.ShapeDtypeStruct((D_MODEL, D_MODEL), _BF),
            jax.ShapeDtypeStruct((D_MODEL, D_MODEL), _BF),
            jax.ShapeDtypeStruct((D_MODEL, D_MODEL), _BF),
        ],
        compiler_params=pltpu.CompilerParams(
            dimension_semantics=("arbitrary",), vmem_limit_bytes=VMEM_LIMIT),
        name="prep",
    )(w_in, w_out_a)


def _proj_kernel(x_ref, nm_ref, wc_ref, wqt_ref, wk_ref, wvt_ref, wg_ref, cw_ref, qn_ref, kn_ref, woa_ref,
                 wup32_ref, wob32_ref, wo32_ref, wdn32_ref,
                 sa_ref, sgb_ref, qt_ref, k_ref, vt_ref, bias_ref, wup_ref, wob_ref, wo_ref, wdn_ref,
                 halo_ref, kbar_ref):
    i = pl.program_id(0)

    @pl.when(i == 0)
    def _():
        halo_ref[...] = jnp.zeros_like(halo_ref)
        kbar_ref[...] = jnp.zeros_like(kbar_ref)

    xb = (_rms(x_ref[...]) * nm_ref[...]).astype(_BF)
    n_blocks = kbar_ref.shape[0]

    zk = _dot(xb, wk_ref[...])
    zqt = _dot_nt(wqt_ref[...], xb)
    zc = _dot(xb, wc_ref[...])

    gates = []
    for h in range(N_HEADS):
        lo = h * HEAD_DIM
        qh = (_rms(zqt[lo:lo + HEAD_DIM, :], axis=0) * qn_ref[...] * QK_SCALE).astype(_BF)
        qt_ref[lo:lo + HEAD_DIM, :] = qh
        gates.append(_dot(kbar_ref[:, lo:lo + HEAD_DIM].astype(_BF), qh))

    zg = _dot(xb, wg_ref[...])

    for h in range(N_HEADS):
        gate = gates[h]
        blk = lax.broadcasted_iota(jnp.int32, gate.shape, 0)
        gate = jnp.where(blk < i, gate, -jnp.inf)
        bias = jnp.full(gate.shape, NEG_BIG, _F32)
        for _ in range(MOBA_TOPK):
            top = jnp.max(gate, axis=0, keepdims=True)
            first = jnp.min(jnp.where(gate == top, blk, n_blocks), axis=0, keepdims=True)
            hit = blk == first
            bias = jnp.where(hit, 0.0, bias)
            gate = jnp.where(hit, -jnp.inf, gate)
        bias_ref[h] = jnp.where(blk < i, bias, NEG_BIG).astype(_BF)

    kbar_rows = []
    for h in range(N_HEADS):
        lo = h * HEAD_DIM
        kh = _rms(zk[:, lo:lo + HEAD_DIM]) * kn_ref[...]
        k_ref[:, lo:lo + HEAD_DIM] = kh.astype(_BF)
        kbar_rows.append(jnp.mean(kh, axis=0, keepdims=True))
    kbar_all = kbar_ref[...]
    blk_row = lax.broadcasted_iota(jnp.int32, kbar_all.shape, 0)
    kbar_ref[...] = jnp.where(blk_row == i, jnp.concatenate(kbar_rows, axis=1), kbar_all)

    bg = zc[:, :D_MODEL]
    c = zc[:, D_MODEL:2 * D_MODEL] * zc[:, 2 * D_MODEL:]
    y = _causal_conv3(c, halo_ref[...], cw_ref[...])
    halo_ref[...] = c[PROJ_ROWS - CONV_HALO_ROWS:, :]
    branch_a = _dot((bg * y).astype(_BF), woa_ref[...])

    vt = _dot_nt(wvt_ref[...], xb)

    sa_ref[...] = jax.nn.sigmoid(zg[:, :D_MODEL]) * branch_a
    sgb_ref[...] = jax.nn.sigmoid(zg[:, D_MODEL:])
    for h in range(N_HEADS):
        vt_ref[h, 0] = vt[h * HEAD_DIM:(h + 1) * HEAD_DIM, :].astype(_BF)

    wup_ref[...] = wup32_ref[...].astype(_BF)
    wob_ref[...] = wob32_ref[...].astype(_BF)
    wo_ref[...] = wo32_ref[...].astype(_BF)

    @pl.when(i < WDN_CAST_STEPS)
    def _():
        wdn_ref[...] = wdn32_ref[...].astype(_BF)


def _proj(x, nm, wb, wqt, wvt, cw, qn, kn, woa, w_up, w_out_b, w_o, w_down):
    s = x.shape[0]
    assert PROJ_ROWS == MOBA_BLOCK
    nb = s // PROJ_ROWS
    cast_rows = D_MODEL // nb
    wdn_rows = D_FF // WDN_CAST_STEPS
    assert cast_rows * nb == D_MODEL and cast_rows % 16 == 0 and wdn_rows % 16 == 0 and nb >= WDN_CAST_STEPS
    const = lambda shape: pl.BlockSpec(shape, lambda i: (0,) * len(shape), pipeline_mode=pl.Buffered(1))
    w_in_cols = lambda width, blk: pl.BlockSpec((D_MODEL, width), lambda i: (0, blk),
                                                pipeline_mode=pl.Buffered(1))
    rows = lambda w: pl.BlockSpec((PROJ_ROWS, w), lambda i: (i, 0))
    slab = lambda w: pl.BlockSpec((cast_rows, w), lambda i: (i, 0))
    wdn_slab = pl.BlockSpec((wdn_rows, D_MODEL), lambda i: (jnp.minimum(i, WDN_CAST_STEPS - 1), 0))
    return pl.pallas_call(
        _proj_kernel,
        grid=(nb,),
        in_specs=[
            rows(D_MODEL),
            const((1, D_MODEL)),
            w_in_cols(3 * D_MODEL, 0),
            const((D_MODEL, D_MODEL)),
            w_in_cols(D_MODEL, 4),
            const((D_MODEL, D_MODEL)),
            w_in_cols(2 * D_MODEL, 3),
            const((3, D_MODEL)),
            const((HEAD_DIM, 1)),
            const((1, HEAD_DIM)),
            const((D_MODEL, D_MODEL)),
            slab(2 * D_FF),
            slab(D_MODEL),
            slab(D_MODEL),
            wdn_slab,
        ],
        out_specs=[
            rows(D_MODEL),
            rows(D_MODEL),
            pl.BlockSpec((D_MODEL, PROJ_ROWS), lambda i: (0, i)),
            rows(D_MODEL),
            pl.BlockSpec((N_HEADS, 1, HEAD_DIM, MOBA_BLOCK), lambda i: (0, i, 0, 0)),
            pl.BlockSpec((N_HEADS, nb, PROJ_ROWS), lambda i: (0, 0, i)),
            slab(2 * D_FF),
            slab(D_MODEL),
            slab(D_MODEL),
            wdn_slab,
        ],
        out_shape=[
            jax.ShapeDtypeStruct((s, D_MODEL), _F32),
            jax.ShapeDtypeStruct((s, D_MODEL), _F32),
            jax.ShapeDtypeStruct((D_MODEL, s), _BF),
            jax.ShapeDtypeStruct((s, D_MODEL), _BF),
            jax.ShapeDtypeStruct((N_HEADS, nb, HEAD_DIM, MOBA_BLOCK), _BF),
            jax.ShapeDtypeStruct((N_HEADS, nb, s), _BF),
            jax.ShapeDtypeStruct((D_MODEL, 2 * D_FF), _BF),
            jax.ShapeDtypeStruct((D_MODEL, D_MODEL), _BF),
            jax.ShapeDtypeStruct((D_MODEL, D_MODEL), _BF),
            jax.ShapeDtypeStruct((D_FF, D_MODEL), _BF),
        ],
        scratch_shapes=[
            pltpu.VMEM((CONV_HALO_ROWS, D_MODEL), _F32),
            pltpu.VMEM((nb, D_MODEL), _F32),
        ],
        compiler_params=pltpu.CompilerParams(
            dimension_semantics=("arbitrary",), vmem_limit_bytes=VMEM_LIMIT),
        name="proj",
    )(x, nm, wb, wqt, wb, wvt, wb, cw, qn, kn, woa, w_up, w_out_b, w_o, w_down)


def _attn_kernel(qt_ref, kblk_ref, vtblk_ref, bias_ref, o_ref, k_ref, vt_ref, qaug_ref, sa_ref, sb_ref,
                 mxa_ref, mxb_ref, m_ref, l_ref, acc_ref, *, n_blocks):
    a = pl.program_id(1)
    own0 = Q_BLOCKS * a
    own1 = own0 + 1
    for b in range(Q_BLOCKS):
        k_ref[own0 + b] = kblk_ref[b]
        for hh in range(ATTN_HEADS):
            vt_ref[hh, own0 + b] = vtblk_ref[hh, b]
    ones_rows = jnp.ones((ONES_ROWS, MOBA_BLOCK), _BF)
    pad_rows = jnp.zeros((QK_DEPTH - HEAD_DIM - n_blocks, Q_TILE), _BF)
    lane = lax.broadcasted_iota(jnp.int32, (ONES_ROWS, HEAD_DIM), 1)

    def head_rows(hh):
        return slice(hh * HEAD_DIM, (hh + 1) * HEAD_DIM)

    def unit_blocks(u):
        js = [UNIT_BLOCKS * (u - 1) + b for b in range(UNIT_BLOCKS)]
        return [jnp.where(j < own0, j, own1) for j in js]

    def keys_with_onehot(j, hh, bias_row):
        row = (lane == bias_row).astype(_BF)
        onehot = jnp.concatenate([row] * (MOBA_BLOCK // ONES_ROWS), axis=0)
        return jnp.concatenate([k_ref[j, :, head_rows(hh)], onehot], axis=1)

    def put_scores(hh, s, dst_ref, mx_ref):
        dst_ref[hh] = s
        mx_ref[hh] = jnp.max(s, axis=0, keepdims=True)

    def scores(u, hh, dst_ref, mx_ref):
        ka = jnp.concatenate([keys_with_onehot(j, hh, j) for j in unit_blocks(u)], axis=0)
        put_scores(hh, _dot(ka, qaug_ref[hh]), dst_ref, mx_ref)

    def first_scores(hh, dst_ref, mx_ref):
        ka = jnp.concatenate([keys_with_onehot(own0, hh, n_blocks), keys_with_onehot(own1, hh, n_blocks)], axis=0)
        s = _dot(ka, qaug_ref[hh])
        kpos = lax.broadcasted_iota(jnp.int32, (MOBA_BLOCK, Q_TILE), 0)
        qcol = lax.broadcasted_iota(jnp.int32, (1, Q_TILE), 1)
        bias = bias_ref[hh].astype(_F32)
        blk = lax.broadcasted_iota(jnp.int32, bias.shape, 0)
        picked = jnp.max(jnp.where(blk == own0, bias, NEG_BIG), axis=0, keepdims=True) >= 0.0
        last0 = jnp.where(qcol < MOBA_BLOCK, qcol, jnp.where(picked, MOBA_BLOCK, -1))
        last1 = qcol - MOBA_BLOCK
        s = jnp.concatenate([jnp.where(kpos <= last0, s[:MOBA_BLOCK], NEG_BIG),
                             jnp.where(kpos <= last1, s[MOBA_BLOCK:], NEG_BIG)], axis=0)
        put_scores(hh, s, dst_ref, mx_ref)

    def update(blocks, hh, src_ref, mx_ref):
        m_old = m_ref[hh]
        m_new = jnp.maximum(m_old, mx_ref[hh])
        alpha = jnp.exp2(m_old - m_new)
        outs = []
        for b, j in enumerate(blocks):
            p = jnp.exp2(src_ref[hh, b * MOBA_BLOCK:(b + 1) * MOBA_BLOCK, :] - m_new).astype(_BF)
            outs.append(_dot(jnp.concatenate([vt_ref[hh, j], ones_rows], axis=0), p))
        out = functools.reduce(jnp.add, outs)
        m_ref[hh] = m_new
        l_ref[hh] = alpha * l_ref[hh] + out[HEAD_DIM:HEAD_DIM + 1]
        acc_ref[hh] = alpha * acc_ref[hh] + out[:HEAD_DIM]

    for hh in range(ATTN_HEADS):
        qaug_ref[hh] = jnp.concatenate([qt_ref[head_rows(hh), :], bias_ref[hh], pad_rows], axis=0)
        m_ref[hh] = jnp.full(m_ref.shape[1:], NEG_BIG, _F32)
        l_ref[hh] = jnp.zeros(l_ref.shape[1:], _F32)
        acc_ref[hh] = jnp.zeros(acc_ref.shape[1:], _F32)
    for hh in range(ATTN_HEADS):
        first_scores(hh, sa_ref, mxa_ref)

    def unit_pairs(u0, n_pairs, first=False):
        for pair in range(n_pairs):
            u = u0 + 2 * pair
            for hh in range(ATTN_HEADS):
                scores(u + 1, hh, sb_ref, mxb_ref)
                update([own0, own1] if first and pair == 0 else unit_blocks(u), hh, sa_ref, mxa_ref)
            for hh in range(ATTN_HEADS):
                scores(u + 2, hh, sa_ref, mxa_ref)
                update(unit_blocks(u + 1), hh, sb_ref, mxb_ref)

    unit_pairs(0, 1, first=True)
    left = jnp.maximum(a - 1, 0)
    more_pairs = lax.shift_right_logical(left, 1)
    long_trips = lax.shift_right_logical(more_pairs, LONG_TRIP_SHIFT)

    def long_body(t, carry):
        unit_pairs(2 + 2 * LONG_TRIP_PAIRS * t, LONG_TRIP_PAIRS)
        return carry

    def short_body(t, carry):
        unit_pairs(2 + 2 * (LONG_TRIP_PAIRS * long_trips + t), 1)
        return carry

    def last_unit_body(t, carry):
        for hh in range(ATTN_HEADS):
            update(unit_blocks(2 + 2 * more_pairs), hh, sa_ref, mxa_ref)
        return carry

    lax.fori_loop(0, long_trips, long_body, 0)
    lax.fori_loop(0, more_pairs - LONG_TRIP_PAIRS * long_trips, short_body, 0)
    lax.fori_loop(0, left - 2 * more_pairs, last_unit_body, 0)
    for hh in range(ATTN_HEADS):
        o_ref[:, head_rows(hh)] = (acc_ref[hh] * (1.0 / l_ref[hh])).astype(o_ref.dtype).T


def _attn(qt, k, vt, bias):
    s = k.shape[0]
    nb = s // MOBA_BLOCK
    assert nb < QK_DEPTH - HEAD_DIM and nb % Q_BLOCKS == 0 and Q_BLOCKS == 2
    k3 = k.reshape(nb, MOBA_BLOCK, D_MODEL)
    gw = ATTN_HEADS * HEAD_DIM
    scores_buf = pltpu.VMEM((ATTN_HEADS, UNIT_KEYS, Q_TILE), _F32)
    row_buf = pltpu.VMEM((ATTN_HEADS, 1, Q_TILE), _F32)
    return pl.pallas_call(
        functools.partial(_attn_kernel, n_blocks=nb),
        grid=(N_HEADS // ATTN_HEADS, nb // Q_BLOCKS),
        in_specs=[
            pl.BlockSpec((gw, Q_TILE), lambda g, a: (g, a)),
            pl.BlockSpec((Q_BLOCKS, MOBA_BLOCK, gw), lambda g, a: (a, 0, g)),
            pl.BlockSpec((ATTN_HEADS, Q_BLOCKS, HEAD_DIM, MOBA_BLOCK), lambda g, a: (g, a, 0, 0)),
            pl.BlockSpec((ATTN_HEADS, nb, Q_TILE), lambda g, a: (g, 0, a)),
        ],
        out_specs=pl.BlockSpec((Q_TILE, gw), lambda g, a: (a, g)),
        out_shape=jax.ShapeDtypeStruct((s, D_MODEL), _BF),
        scratch_shapes=[
            pltpu.VMEM((nb, MOBA_BLOCK, gw), _BF),
            pltpu.VMEM((ATTN_HEADS, nb, HEAD_DIM, MOBA_BLOCK), _BF),
            pltpu.VMEM((ATTN_HEADS, QK_DEPTH, Q_TILE), _BF),
            scores_buf,
            scores_buf,
            row_buf,
            row_buf,
            row_buf,
            row_buf,
            pltpu.VMEM((ATTN_HEADS, HEAD_DIM, Q_TILE), _F32),
        ],
        compiler_params=pltpu.CompilerParams(
            dimension_semantics=("arbitrary", "arbitrary"), vmem_limit_bytes=VMEM_LIMIT),
        name="moba_attn",
    )(qt, k3, vt, bias)


def _post_kernel(o_ref, sa_ref, sgb_ref, x_ref, wob_ref, wo_ref, nf_ref, wup_ref, fcw_ref, fcb_ref,
                 wdn_ref, out_ref, halo_ref, ubuf_ref):
    i = pl.program_id(0)

    @pl.when(i == 0)
    def _():
        halo_ref[...] = jnp.zeros_like(halo_ref)

    branch_b = _dot(o_ref[...], wob_ref[...])
    merged = sa_ref[...] + sgb_ref[...] * branch_b
    h = x_ref[...] + _dot(merged.astype(_BF), wo_ref[...])
    hn = (_rms(h) * nf_ref[...]).astype(_BF)

    n_chunks = D_FF // FFN_CHUNK

    def up_proj(c):
        for part in range(2):
            lo = part * D_FF + c * FFN_CHUNK
            u = _dot(hn, wup_ref[:, lo:lo + FFN_CHUNK])
            buf = ubuf_ref.at[c % (FFN_LOOKAHEAD + 1), part]
            buf[:CONV_HALO_ROWS, :] = halo_ref[:, lo:lo + FFN_CHUNK]
            buf[CONV_HALO_ROWS:, :] = u
            halo_ref[:, lo:lo + FFN_CHUNK] = u[POST_ROWS - CONV_HALO_ROWS:, :]

    def conv(c, part):
        lo = part * D_FF + c * FFN_CHUNK
        buf = ubuf_ref.at[c % (FFN_LOOKAHEAD + 1), part]
        w = fcw_ref[:, lo:lo + FFN_CHUNK]
        taps = [buf[CONV_HALO_ROWS - 2 + j:CONV_HALO_ROWS - 2 + j + POST_ROWS, :] for j in range(3)]
        return (w[0:1, :] * taps[0] + w[1:2, :] * taps[1] + w[2:3, :] * taps[2]
                + fcb_ref[:, lo:lo + FFN_CHUNK])

    acc = h
    for c in range(min(FFN_LOOKAHEAD, n_chunks)):
        up_proj(c)
    for c in range(n_chunks):
        if c + FFN_LOOKAHEAD < n_chunks:
            up_proj(c + FFN_LOOKAHEAD)
        half_g = 0.5 * conv(c, 0)
        act = ((half_g + half_g * jnp.tanh(half_g)) * conv(c, 1)).astype(_BF)
        acc = acc + _dot(act, wdn_ref[c * FFN_CHUNK:(c + 1) * FFN_CHUNK, :])
    out_ref[...] = acc


def _post(o, sa, sgb, x, wob, wo, nf, wup, fcw, fcb, wdn):
    s = x.shape[0]
    const = lambda shape: pl.BlockSpec(shape, lambda i: (0,) * len(shape), pipeline_mode=pl.Buffered(1))
    rows = pl.BlockSpec((POST_ROWS, D_MODEL), lambda i: (i, 0))
    return pl.pallas_call(
        _post_kernel,
        grid=(s // POST_ROWS,),
        in_specs=[
            rows, rows, rows, rows,
            const((D_MODEL, D_MODEL)),
            const((D_MODEL, D_MODEL)),
            const((1, D_MODEL)),
            const((D_MODEL, 2 * D_FF)),
            const((3, 2 * D_FF)),
            const((1, 2 * D_FF)),
            const((D_FF, D_MODEL)),
        ],
        out_specs=rows,
        out_shape=jax.ShapeDtypeStruct((s, D_MODEL), _F32),
        scratch_shapes=[
            pltpu.VMEM((CONV_HALO_ROWS, 2 * D_FF), _F32),
            pltpu.VMEM((FFN_LOOKAHEAD + 1, 2, CONV_HALO_ROWS + POST_ROWS, FFN_CHUNK), _F32),
        ],
        compiler_params=pltpu.CompilerParams(
            dimension_semantics=("arbitrary",), vmem_limit_bytes=VMEM_LIMIT),
        name="post",
    )(o, sa, sgb, x, wob, wo, nf, wup, fcw, fcb, wdn)


def kernel(x, norm_mix, w_in, conv_a_w, q_norm, k_norm, w_out_a, w_out_b, w_o, norm_ffn, w_up,
           ffn_conv_w, ffn_conv_b, w_down):
    b, s, d = x.shape
    assert b == 1 and d == D_MODEL and s % POST_ROWS == 0 and s % MOBA_BLOCK == 0
    x2 = x.reshape(s, d)
    wb, wqt, wvt, woa = _prep(w_in, w_out_a)

    sa, sgb, qt, k, vt, bias, wup, wob, wo, wdn = _proj(
        x2, norm_mix.reshape(1, d), wb, wqt, wvt, conv_a_w, q_norm.reshape(HEAD_DIM, 1),
        k_norm.reshape(1, HEAD_DIM), woa, w_up, w_out_b, w_o, w_down)
    o = _attn(qt, k, vt, bias)
    out = _post(o, sa, sgb, x2, wob, wo, norm_ffn.reshape(1, d), wup, ffn_conv_w,
                ffn_conv_b.reshape(1, 2 * D_FF), wdn)
    return out.reshape(b, s, d)
```

```python
import functools

import jax
import jax.numpy as jnp
from jax import lax
from jax.experimental import pallas as pl
from jax.experimental.pallas import tpu as pltpu

D_MODEL = 1024
N_HEADS = 8
HEAD_DIM = 128
MOBA_BLOCK = 256
MOBA_TOPK = 3
D_FF = 2816
EPS = 1e-6
CONV_HALO_ROWS = 8
NEG_BIG = -1e30

W_IN_Q_GROUP = 3
W_IN_V_GROUP = 5
PROJ_ROWS = 256
WDN_CAST_STEPS = 16
POST_ROWS = 512
FFN_CHUNK = 256
FFN_LOOKAHEAD = 3
VMEM_LIMIT = 56 * 1024 * 1024
ATTN_HEADS = 4
ONES_ROWS = 16
Q_BLOCKS = 2
UNIT_BLOCKS = Q_BLOCKS
UNIT_KEYS = UNIT_BLOCKS * MOBA_BLOCK
Q_TILE = Q_BLOCKS * MOBA_BLOCK
LONG_TRIP_SHIFT = 2
LONG_TRIP_PAIRS = 1 << LONG_TRIP_SHIFT
QK_DEPTH = 2 * HEAD_DIM
QK_SCALE = HEAD_DIM ** -0.5 * 1.4426950408889634

_BF = jnp.bfloat16
_F32 = jnp.float32


def _dot(a, b):
    return jnp.dot(a, b, preferred_element_type=_F32)


def _dot_nt(a, b):
    return lax.dot_general(a, b, (((1,), (1,)), ((), ())), preferred_element_type=_F32)


def _rms(x, axis=-1):
    return x * lax.rsqrt(jnp.mean(x * x, axis=axis, keepdims=True) + EPS)


def _causal_conv3(c, prev, w):
    row = lax.broadcasted_iota(jnp.int32, c.shape, 0)
    p1 = prev[CONV_HALO_ROWS - 1:CONV_HALO_ROWS, :]
    p2 = prev[CONV_HALO_ROWS - 2:CONV_HALO_ROWS - 1, :]
    c1 = jnp.where(row == 0, p1, pltpu.roll(c, 1, 0))
    c2 = jnp.where(row == 0, p2, jnp.where(row == 1, p1, pltpu.roll(c, 2, 0)))
    return w[0:1, :] * c2 + w[1:2, :] * c1 + w[2:3, :] * c


def _prep_kernel(w_ref, woa32_ref, wb_ref, wqt_ref, wvt_ref, woa_ref):
    c = pl.program_id(0)
    wb_ref[...] = w_ref[...].astype(_BF)
    woa_ref[...] = woa32_ref[...].astype(_BF)

    @pl.when(c == W_IN_Q_GROUP)
    def _():
        wqt_ref[...] = w_ref[...].T.astype(_BF)

    @pl.when(c == W_IN_V_GROUP)
    def _():
        wvt_ref[...] = w_ref[...].T.astype(_BF)


def _prep(w_in, w_out_a):
    groups = w_in.shape[1] // D_MODEL
    slab_rows = D_MODEL // groups
    assert slab_rows * groups == D_MODEL and slab_rows % 16 == 0
    square = lambda index_map: pl.BlockSpec((D_MODEL, D_MODEL), index_map)
    slab = pl.BlockSpec((slab_rows, D_MODEL), lambda c: (c, 0))
    return pl.pallas_call(
        _prep_kernel,
        grid=(groups,),
        in_specs=[square(lambda c: (0, c)), slab],
        out_specs=[square(lambda c: (0, c)), square(lambda c: (0, 0)), square(lambda c: (0, 0)), slab],
        out_shape=[
            jax.ShapeDtypeStruct(w_in.shape, _BF),
            jax.ShapeDtypeStruct((D_MODEL, D_MODEL), _BF),
            jax.ShapeDtypeStruct((D_MODEL, D_MODEL), _BF),
            jax.ShapeDtypeStruct((D_MODEL, D_MODEL), _BF),
        ],
        compiler_params=pltpu.CompilerParams(
            dimension_semantics=("arbitrary",), vmem_limit_bytes=VMEM_LIMIT),
        name="prep",
    )(w_in, w_out_a)


def _proj_kernel(x_ref, nm_ref, wc_ref, wqt_ref, wk_ref, wvt_ref, wg_ref, cw_ref, qn_ref, kn_ref, woa_ref,
                 wup32_ref, wob32_ref, wo32_ref, wdn32_ref,
                 sa_ref, sgb_ref, qt_ref, k_ref, vt_ref, bias_ref, wup_ref, wob_ref, wo_ref, wdn_ref,
                 halo_ref, kbar_ref):
    i = pl.program_id(0)

    @pl.when(i == 0)
    def _():
        halo_ref[...] = jnp.zeros_like(halo_ref)
        kbar_ref[...] = jnp.zeros_like(kbar_ref)

    xb = (_rms(x_ref[...]) * nm_ref[...]).astype(_BF)
    n_blocks = kbar_ref.shape[0]

    zk = _dot(xb, wk_ref[...])
    zqt = _dot_nt(wqt_ref[...], xb)
    zc = _dot(xb, wc_ref[...])

    gates = []
    for h in range(N_HEADS):
        lo = h * HEAD_DIM
        qh = (_rms(zqt[lo:lo + HEAD_DIM, :], axis=0) * qn_ref[...] * QK_SCALE).astype(_BF)
        qt_ref[lo:lo + HEAD_DIM, :] = qh
        gates.append(_dot(kbar_ref[:, lo:lo + HEAD_DIM].astype(_BF), qh))

    zg = _dot(xb, wg_ref[...])

    for h in range(N_HEADS):
        gate = gates[h]
        blk = lax.broadcasted_iota(jnp.int32, gate.shape, 0)
        gate = jnp.where(blk < i, gate, -jnp.inf)
        bias = jnp.full(gate.shape, NEG_BIG, _F32)
        for _ in range(MOBA_TOPK):
            top = jnp.max(gate, axis=0, keepdims=True)
            first = jnp.min(jnp.where(gate == top, blk, n_blocks), axis=0, keepdims=True)
            hit = blk == first
            bias = jnp.where(hit, 0.0, bias)
            gate = jnp.where(hit, -jnp.inf, gate)
        bias_ref[h] = jnp.where(blk < i, bias, NEG_BIG).astype(_BF)

    kbar_rows = []
    for h in range(N_HEADS):
        lo = h * HEAD_DIM
        kh = _rms(zk[:, lo:lo + HEAD_DIM]) * kn_ref[...]
        k_ref[:, lo:lo + HEAD_DIM] = kh.astype(_BF)
        kbar_rows.append(jnp.mean(kh, axis=0, keepdims=True))
    kbar_all = kbar_ref[...]
    blk_row = lax.broadcasted_iota(jnp.int32, kbar_all.shape, 0)
    kbar_ref[...] = jnp.where(blk_row == i, jnp.concatenate(kbar_rows, axis=1), kbar_all)

    bg = zc[:, :D_MODEL]
    c = zc[:, D_MODEL:2 * D_MODEL] * zc[:, 2 * D_MODEL:]
    y = _causal_conv3(c, halo_ref[...], cw_ref[...])
    halo_ref[...] = c[PROJ_ROWS - CONV_HALO_ROWS:, :]
    branch_a = _dot((bg * y).astype(_BF), woa_ref[...])

    vt = _dot_nt(wvt_ref[...], xb)

    sa_ref[...] = jax.nn.sigmoid(zg[:, :D_MODEL]) * branch_a
    sgb_ref[...] = jax.nn.sigmoid(zg[:, D_MODEL:])
    for h in range(N_HEADS):
        vt_ref[h, 0] = vt[h * HEAD_DIM:(h + 1) * HEAD_DIM, :].astype(_BF)

    wup_ref[...] = wup32_ref[...].astype(_BF)
    wob_ref[...] = wob32_ref[...].astype(_BF)
    wo_ref[...] = wo32_ref[...].astype(_BF)

    @pl.when(i < WDN_CAST_STEPS)
    def _():
        wdn_ref[...] = wdn32_ref[...].astype(_BF)


def _proj(x, nm, wb, wqt, wvt, cw, qn, kn, woa, w_up, w_out_b, w_o, w_down):
    s = x.shape[0]
    assert PROJ_ROWS == MOBA_BLOCK
    nb = s // PROJ_ROWS
    cast_rows = D_MODEL // nb
    wdn_rows = D_FF // WDN_CAST_STEPS
    assert cast_rows * nb == D_MODEL and cast_rows % 16 == 0 and wdn_rows % 16 == 0 and nb >= WDN_CAST_STEPS
    const = lambda shape: pl.BlockSpec(shape, lambda i: (0,) * len(shape), pipeline_mode=pl.Buffered(1))
    w_in_cols = lambda width, blk: pl.BlockSpec((D_MODEL, width), lambda i: (0, blk),
                                                pipeline_mode=pl.Buffered(1))
    rows = lambda w: pl.BlockSpec((PROJ_ROWS, w), lambda i: (i, 0))
    slab = lambda w: pl.BlockSpec((cast_rows, w), lambda i: (i, 0))
    wdn_slab = pl.BlockSpec((wdn_rows, D_MODEL), lambda i: (jnp.minimum(i, WDN_CAST_STEPS - 1), 0))
    return pl.pallas_call(
        _proj_kernel,
        grid=(nb,),
        in_specs=[
            rows(D_MODEL),
            const((1, D_MODEL)),
            w_in_cols(3 * D_MODEL, 0),
            const((D_MODEL, D_MODEL)),
            w_in_cols(D_MODEL, 4),
            const((D_MODEL, D_MODEL)),
            w_in_cols(2 * D_MODEL, 3),
            const((3, D_MODEL)),
            const((HEAD_DIM, 1)),
            const((1, HEAD_DIM)),
            const((D_MODEL, D_MODEL)),
            slab(2 * D_FF),
            slab(D_MODEL),
            slab(D_MODEL),
            wdn_slab,
        ],
        out_specs=[
            rows(D_MODEL),
            rows(D_MODEL),
            pl.BlockSpec((D_MODEL, PROJ_ROWS), lambda i: (0, i)),
            rows(D_MODEL),
            pl.BlockSpec((N_HEADS, 1, HEAD_DIM, MOBA_BLOCK), lambda i: (0, i, 0, 0)),
            pl.BlockSpec((N_HEADS, nb, PROJ_ROWS), lambda i: (0, 0, i)),
            slab(2 * D_FF),
            slab(D_MODEL),
            slab(D_MODEL),
            wdn_slab,
        ],
        out_shape=[
            jax.ShapeDtypeStruct((s, D_MODEL), _F32),
            jax.ShapeDtypeStruct((s, D_MODEL), _F32),
            jax.ShapeDtypeStruct((D_MODEL, s), _BF),
            jax.ShapeDtypeStruct((s, D_MODEL), _BF),
            jax.ShapeDtypeStruct((N_HEADS, nb, HEAD_DIM, MOBA_BLOCK), _BF),
            jax.ShapeDtypeStruct((N_HEADS, nb, s), _BF),
            jax.ShapeDtypeStruct((D_MODEL, 2 * D_FF), _BF),
            jax.ShapeDtypeStruct((D_MODEL, D_MODEL), _BF),
            jax.ShapeDtypeStruct((D_MODEL, D_MODEL), _BF),
            jax.ShapeDtypeStruct((D_FF, D_MODEL), _BF),
        ],
        scratch_shapes=[
            pltpu.VMEM((CONV_HALO_ROWS, D_MODEL), _F32),
            pltpu.VMEM((nb, D_MODEL), _F32),
        ],
        compiler_params=pltpu.CompilerParams(
            dimension_semantics=("arbitrary",), vmem_limit_bytes=VMEM_LIMIT),
        name="proj",
    )(x, nm, wb, wqt, wb, wvt, wb, cw, qn, kn, woa, w_up, w_out_b, w_o, w_down)


def _attn_kernel(qt_ref, kblk_ref, vtblk_ref, bias_ref, o_ref, k_ref, vt_ref, qaug_ref, sa_ref, sb_ref,
                 mxa_ref, mxb_ref, m_ref, l_ref, acc_ref, *, n_blocks):
    a = pl.program_id(1)
    own0 = Q_BLOCKS * a
    own1 = own0 + 1
    for b in range(Q_BLOCKS):
        k_ref[own0 + b] = kblk_ref[b]
        for hh in range(ATTN_HEADS):
            vt_ref[hh, own0 + b] = vtblk_ref[hh, b]
    ones_rows = jnp.ones((ONES_ROWS, MOBA_BLOCK), _BF)
    pad_rows = jnp.zeros((QK_DEPTH - HEAD_DIM - n_blocks, Q_TILE), _BF)
    lane = lax.broadcasted_iota(jnp.int32, (ONES_ROWS, HEAD_DIM), 1)

    def head_rows(hh):
        return slice(hh * HEAD_DIM, (hh + 1) * HEAD_DIM)

    def unit_blocks(u):
        js = [UNIT_BLOCKS * (u - 1) + b for b in range(UNIT_BLOCKS)]
        return [jnp.where(j < own0, j, own1) for j in js]

    def keys_with_onehot(j, hh, bias_row):
        row = (lane == bias_row).astype(_BF)
        onehot = jnp.concatenate([row] * (MOBA_BLOCK // ONES_ROWS), axis=0)
        return jnp.concatenate([k_ref[j, :, head_rows(hh)], onehot], axis=1)

    def put_scores(hh, s, dst_ref, mx_ref):
        dst_ref[hh] = s
        mx_ref[hh] = jnp.max(s, axis=0, keepdims=True)

    def scores(u, hh, dst_ref, mx_ref):
        ka = jnp.concatenate([keys_with_onehot(j, hh, j) for j in unit_blocks(u)], axis=0)
        put_scores(hh, _dot(ka, qaug_ref[hh]), dst_ref, mx_ref)

    def first_scores(hh, dst_ref, mx_ref):
        ka = jnp.concatenate([keys_with_onehot(own0, hh, n_blocks), keys_with_onehot(own1, hh, n_blocks)], axis=0)
        s = _dot(ka, qaug_ref[hh])
        kpos = lax.broadcasted_iota(jnp.int32, (MOBA_BLOCK, Q_TILE), 0)
        qcol = lax.broadcasted_iota(jnp.int32, (1, Q_TILE), 1)
        bias = bias_ref[hh].astype(_F32)
        blk = lax.broadcasted_iota(jnp.int32, bias.shape, 0)
        picked = jnp.max(jnp.where(blk == own0, bias, NEG_BIG), axis=0, keepdims=True) >= 0.0
        last0 = jnp.where(qcol < MOBA_BLOCK, qcol, jnp.where(picked, MOBA_BLOCK, -1))
        last1 = qcol - MOBA_BLOCK
        s = jnp.concatenate([jnp.where(kpos <= last0, s[:MOBA_BLOCK], NEG_BIG),
                             jnp.where(kpos <= last1, s[MOBA_BLOCK:], NEG_BIG)], axis=0)
        put_scores(hh, s, dst_ref, mx_ref)

    def update(blocks, hh, src_ref, mx_ref):
        m_old = m_ref[hh]
        m_new = jnp.maximum(m_old, mx_ref[hh])
        alpha = jnp.exp2(m_old - m_new)
        outs = []
        for b, j in enumerate(blocks):
            p = jnp.exp2(src_ref[hh, b * MOBA_BLOCK:(b + 1) * MOBA_BLOCK, :] - m_new).astype(_BF)
            outs.append(_dot(jnp.concatenate([vt_ref[hh, j], ones_rows], axis=0), p))
        out = functools.reduce(jnp.add, outs)
        m_ref[hh] = m_new
        l_ref[hh] = alpha * l_ref[hh] + out[HEAD_DIM:HEAD_DIM + 1]
        acc_ref[hh] = alpha * acc_ref[hh] + out[:HEAD_DIM]

    for hh in range(ATTN_HEADS):
        qaug_ref[hh] = jnp.concatenate([qt_ref[head_rows(hh), :], bias_ref[hh], pad_rows], axis=0)
        m_ref[hh] = jnp.full(m_ref.shape[1:], NEG_BIG, _F32)
        l_ref[hh] = jnp.zeros(l_ref.shape[1:], _F32)
        acc_ref[hh] = jnp.zeros(acc_ref.shape[1:], _F32)
    for hh in range(ATTN_HEADS):
        first_scores(hh, sa_ref, mxa_ref)

    def unit_pairs(u0, n_pairs, first=False):
        for pair in range(n_pairs):
            u = u0 + 2 * pair
            for hh in range(ATTN_HEADS):
                scores(u + 1, hh, sb_ref, mxb_ref)
                update([own0, own1] if first and pair == 0 else unit_blocks(u), hh, sa_ref, mxa_ref)
            for hh in range(ATTN_HEADS):
                scores(u + 2, hh, sa_ref, mxa_ref)
                update(unit_blocks(u + 1), hh, sb_ref, mxb_ref)

    unit_pairs(0, 1, first=True)
    left = jnp.maximum(a - 1, 0)
    more_pairs = lax.shift_right_logical(left, 1)
    long_trips = lax.shift_right_logical(more_pairs, LONG_TRIP_SHIFT)

    def long_body(t, carry):
        unit_pairs(2 + 2 * LONG_TRIP_PAIRS * t, LONG_TRIP_PAIRS)
        return carry

    pairs_done = LONG_TRIP_PAIRS * long_trips
    mid_trips = lax.shift_right_logical(more_pairs - pairs_done, 1)

    def mid_body(t, carry):
        unit_pairs(2 + 2 * pairs_done, 2)
        return carry

    def short_body(t, carry):
        unit_pairs(2 + 2 * (pairs_done + 2 * mid_trips), 1)
        return carry

    def last_unit_body(t, carry):
        for hh in range(ATTN_HEADS):
            update(unit_blocks(2 + 2 * more_pairs), hh, sa_ref, mxa_ref)
        return carry

    lax.fori_loop(0, long_trips, long_body, 0)
    lax.fori_loop(0, mid_trips, mid_body, 0)
    lax.fori_loop(0, more_pairs - pairs_done - 2 * mid_trips, short_body, 0)
    lax.fori_loop(0, left - 2 * more_pairs, last_unit_body, 0)
    for hh in range(ATTN_HEADS):
        o_ref[:, head_rows(hh)] = (acc_ref[hh] * (1.0 / l_ref[hh])).astype(o_ref.dtype).T


def _attn(qt, k, vt, bias):
    s = k.shape[0]
    nb = s // MOBA_BLOCK
    assert nb < QK_DEPTH - HEAD_DIM and nb % Q_BLOCKS == 0 and Q_BLOCKS == 2
    k3 = k.reshape(nb, MOBA_BLOCK, D_MODEL)
    gw = ATTN_HEADS * HEAD_DIM
    scores_buf = pltpu.VMEM((ATTN_HEADS, UNIT_KEYS, Q_TILE), _F32)
    row_buf = pltpu.VMEM((ATTN_HEADS, 1, Q_TILE), _F32)
    return pl.pallas_call(
        functools.partial(_attn_kernel, n_blocks=nb),
        grid=(N_HEADS // ATTN_HEADS, nb // Q_BLOCKS),
        in_specs=[
            pl.BlockSpec((gw, Q_TILE), lambda g, a: (g, a)),
            pl.BlockSpec((Q_BLOCKS, MOBA_BLOCK, gw), lambda g, a: (a, 0, g)),
            pl.BlockSpec((ATTN_HEADS, Q_BLOCKS, HEAD_DIM, MOBA_BLOCK), lambda g, a: (g, a, 0, 0)),
            pl.BlockSpec((ATTN_HEADS, nb, Q_TILE), lambda g, a: (g, 0, a)),
        ],
        out_specs=pl.BlockSpec((Q_TILE, gw), lambda g, a: (a, g)),
        out_shape=jax.ShapeDtypeStruct((s, D_MODEL), _BF),
        scratch_shapes=[
            pltpu.VMEM((nb, MOBA_BLOCK, gw), _BF),
            pltpu.VMEM((ATTN_HEADS, nb, HEAD_DIM, MOBA_BLOCK), _BF),
            pltpu.VMEM((ATTN_HEADS, QK_DEPTH, Q_TILE), _BF),
            scores_buf,
            scores_buf,
            row_buf,
            row_buf,
            row_buf,
            row_buf,
            pltpu.VMEM((ATTN_HEADS, HEAD_DIM, Q_TILE), _F32),
        ],
        compiler_params=pltpu.CompilerParams(
            dimension_semantics=("arbitrary", "arbitrary"), vmem_limit_bytes=VMEM_LIMIT),
        name="moba_attn",
    )(qt, k3, vt, bias)


def _post_kernel(o_ref, sa_ref, sgb_ref, x_ref, wob_ref, wo_ref, nf_ref, wup_ref, fcw_ref, fcb_ref,
                 wdn_ref, out_ref, halo_ref, ubuf_ref):
    i = pl.program_id(0)

    @pl.when(i == 0)
    def _():
        halo_ref[...] = jnp.zeros_like(halo_ref)

    branch_b = _dot(o_ref[...], wob_ref[...])
    merged = sa_ref[...] + sgb_ref[...] * branch_b
    h = x_ref[...] + _dot(merged.astype(_BF), wo_ref[...])
    hn = (_rms(h) * nf_ref[...]).astype(_BF)

    n_chunks = D_FF // FFN_CHUNK

    def up_proj(c):
        for part in range(2):
            lo = part * D_FF + c * FFN_CHUNK
            u = _dot(hn, wup_ref[:, lo:lo + FFN_CHUNK])
            buf = ubuf_ref.at[c % (FFN_LOOKAHEAD + 1), part]
            buf[:CONV_HALO_ROWS, :] = halo_ref[:, lo:lo + FFN_CHUNK]
            buf[CONV_HALO_ROWS:, :] = u
            halo_ref[:, lo:lo + FFN_CHUNK] = u[POST_ROWS - CONV_HALO_ROWS:, :]

    def conv(c, part):
        lo = part * D_FF + c * FFN_CHUNK
        buf = ubuf_ref.at[c % (FFN_LOOKAHEAD + 1), part]
        w = fcw_ref[:, lo:lo + FFN_CHUNK]
        taps = [buf[CONV_HALO_ROWS - 2 + j:CONV_HALO_ROWS - 2 + j + POST_ROWS, :] for j in range(3)]
        return (w[0:1, :] * taps[0] + w[1:2, :] * taps[1] + w[2:3, :] * taps[2]
                + fcb_ref[:, lo:lo + FFN_CHUNK])

    acc = h
    for c in range(min(FFN_LOOKAHEAD, n_chunks)):
        up_proj(c)
    for c in range(n_chunks):
        if c + FFN_LOOKAHEAD < n_chunks:
            up_proj(c + FFN_LOOKAHEAD)
        half_g = 0.5 * conv(c, 0)
        act = ((half_g + half_g * jnp.tanh(half_g)) * conv(c, 1)).astype(_BF)
        acc = acc + _dot(act, wdn_ref[c * FFN_CHUNK:(c + 1) * FFN_CHUNK, :])
    out_ref[...] = acc


def _post(o, sa, sgb, x, wob, wo, nf, wup, fcw, fcb, wdn):
    s = x.shape[0]
    const = lambda shape: pl.BlockSpec(shape, lambda i: (0,) * len(shape), pipeline_mode=pl.Buffered(1))
    rows = pl.BlockSpec((POST_ROWS, D_MODEL), lambda i: (i, 0))
    return pl.pallas_call(
        _post_kernel,
        grid=(s // POST_ROWS,),
        in_specs=[
            rows, rows, rows, rows,
            const((D_MODEL, D_MODEL)),
            const((D_MODEL, D_MODEL)),
            const((1, D_MODEL)),
            const((D_MODEL, 2 * D_FF)),
            const((3, 2 * D_FF)),
            const((1, 2 * D_FF)),
            const((D_FF, D_MODEL)),
        ],
        out_specs=rows,
        out_shape=jax.ShapeDtypeStruct((s, D_MODEL), _F32),
        scratch_shapes=[
            pltpu.VMEM((CONV_HALO_ROWS, 2 * D_FF), _F32),
            pltpu.VMEM((FFN_LOOKAHEAD + 1, 2, CONV_HALO_ROWS + POST_ROWS, FFN_CHUNK), _F32),
        ],
        compiler_params=pltpu.CompilerParams(
            dimension_semantics=("arbitrary",), vmem_limit_bytes=VMEM_LIMIT),
        name="post",
    )(o, sa, sgb, x, wob, wo, nf, wup, fcw, fcb, wdn)


def kernel(x, norm_mix, w_in, conv_a_w, q_norm, k_norm, w_out_a, w_out_b, w_o, norm_ffn, w_up,
           ffn_conv_w, ffn_conv_b, w_down):
    b, s, d = x.shape
    assert b == 1 and d == D_MODEL and s % POST_ROWS == 0 and s % MOBA_BLOCK == 0
    x2 = x.reshape(s, d)
    wb, wqt, wvt, woa = _prep(w_in, w_out_a)

    sa, sgb, qt, k, vt, bias, wup, wob, wo, wdn = _proj(
        x2, norm_mix.reshape(1, d), wb, wqt, wvt, conv_a_w, q_norm.reshape(HEAD_DIM, 1),
        k_norm.reshape(1, HEAD_DIM), woa, w_up, w_out_b, w_o, w_down)
    o = _attn(qt, k, vt, bias)
    out = _post(o, sa, sgb, x2, wob, wo, norm_ffn.reshape(1, d), wup, ffn_conv_w,
                ffn_conv_b.reshape(1, 2 * D_FF), wdn)
    return out.reshape(b, s, d)
```

```python
import functools

import jax
import jax.numpy as jnp
from jax import lax
from jax.experimental import pallas as pl
from jax.experimental.pallas import tpu as pltpu

D_MODEL = 1024
N_HEADS = 8
HEAD_DIM = 128
MOBA_BLOCK = 256
MOBA_TOPK = 3
D_FF = 2816
EPS = 1e-6
CONV_HALO_ROWS = 8
NEG_BIG = -1e30

W_IN_Q_GROUP = 3
W_IN_V_GROUP = 5
PROJ_ROWS = 256
WDN_CAST_STEPS = 16
POST_ROWS = 512
FFN_CHUNK = 256
ACT_ROWS = 128
FFN_LOOKAHEAD = 3
VMEM_LIMIT = 56 * 1024 * 1024
ATTN_HEADS = 4
ONES_ROWS = 16
Q_BLOCKS = 2
UNIT_BLOCKS = Q_BLOCKS
UNIT_KEYS = UNIT_BLOCKS * MOBA_BLOCK
Q_TILE = Q_BLOCKS * MOBA_BLOCK
LONG_TRIP_SHIFT = 2
LONG_TRIP_PAIRS = 1 << LONG_TRIP_SHIFT
QK_DEPTH = 2 * HEAD_DIM
QK_SCALE = HEAD_DIM ** -0.5 * 1.4426950408889634

_BF = jnp.bfloat16
_F32 = jnp.float32


def _dot(a, b):
    return jnp.dot(a, b, preferred_element_type=_F32)


def _dot_nt(a, b):
    return lax.dot_general(a, b, (((1,), (1,)), ((), ())), preferred_element_type=_F32)


def _rms(x, axis=-1):
    return x * lax.rsqrt(jnp.mean(x * x, axis=axis, keepdims=True) + EPS)


def _causal_conv3(c, prev, w):
    row = lax.broadcasted_iota(jnp.int32, c.shape, 0)
    p1 = prev[CONV_HALO_ROWS - 1:CONV_HALO_ROWS, :]
    p2 = prev[CONV_HALO_ROWS - 2:CONV_HALO_ROWS - 1, :]
    c1 = jnp.where(row == 0, p1, pltpu.roll(c, 1, 0))
    c2 = jnp.where(row == 0, p2, jnp.where(row == 1, p1, pltpu.roll(c, 2, 0)))
    return w[0:1, :] * c2 + w[1:2, :] * c1 + w[2:3, :] * c


def _prep_kernel(w_ref, woa32_ref, wb_ref, wqt_ref, wvt_ref, woa_ref):
    c = pl.program_id(0)
    wb_ref[...] = w_ref[...].astype(_BF)
    woa_ref[...] = woa32_ref[...].astype(_BF)

    @pl.when(c == W_IN_Q_GROUP)
    def _():
        wqt_ref[...] = w_ref[...].T.astype(_BF)

    @pl.when(c == W_IN_V_GROUP)
    def _():
        wvt_ref[...] = w_ref[...].T.astype(_BF)


def _prep(w_in, w_out_a):
    groups = w_in.shape[1] // D_MODEL
    slab_rows = D_MODEL // groups
    assert slab_rows * groups == D_MODEL and slab_rows % 16 == 0
    square = lambda index_map: pl.BlockSpec((D_MODEL, D_MODEL), index_map)
    slab = pl.BlockSpec((slab_rows, D_MODEL), lambda c: (c, 0))
    return pl.pallas_call(
        _prep_kernel,
        grid=(groups,),
        in_specs=[square(lambda c: (0, c)), slab],
        out_specs=[square(lambda c: (0, c)), square(lambda c: (0, 0)), square(lambda c: (0, 0)), slab],
        out_shape=[
            jax.ShapeDtypeStruct(w_in.shape, _BF),
            jax.ShapeDtypeStruct((D_MODEL, D_MODEL), _BF),
            jax.ShapeDtypeStruct((D_MODEL, D_MODEL), _BF),
            jax.ShapeDtypeStruct((D_MODEL, D_MODEL), _BF),
        ],
        compiler_params=pltpu.CompilerParams(
            dimension_semantics=("arbitrary",), vmem_limit_bytes=VMEM_LIMIT),
        name="prep",
    )(w_in, w_out_a)


def _proj_kernel(x_ref, nm_ref, wc_ref, wqt_ref, wk_ref, wvt_ref, wg_ref, cw_ref, qn_ref, kn_ref, woa_ref,
                 wup32_ref, wob32_ref, wo32_ref, wdn32_ref,
                 sa_ref, sgb_ref, qt_ref, k_ref, vt_ref, bias_ref, wup_ref, wob_ref, wo_ref, wdn_ref,
                 halo_ref, kbar_ref):
    i = pl.program_id(0)

    @pl.when(i == 0)
    def _():
        halo_ref[...] = jnp.zeros_like(halo_ref)
        kbar_ref[...] = jnp.zeros_like(kbar_ref)

    xb = (_rms(x_ref[...]) * nm_ref[...]).astype(_BF)
    n_blocks = kbar_ref.shape[0]

    zk = _dot(xb, wk_ref[...])
    zqt = _dot_nt(wqt_ref[...], xb)
    zc = _dot(xb, wc_ref[...])

    gates = []
    for h in range(N_HEADS):
        lo = h * HEAD_DIM
        qh = (_rms(zqt[lo:lo + HEAD_DIM, :], axis=0) * qn_ref[...] * QK_SCALE).astype(_BF)
        qt_ref[lo:lo + HEAD_DIM, :] = qh
        gates.append(_dot(kbar_ref[:, lo:lo + HEAD_DIM].astype(_BF), qh))

    zg = _dot(xb, wg_ref[...])

    for h in range(N_HEADS):
        gate = gates[h]
        blk = lax.broadcasted_iota(jnp.int32, gate.shape, 0)
        gate = jnp.where(blk < i, gate, -jnp.inf)
        bias = jnp.full(gate.shape, NEG_BIG, _F32)
        for _ in range(MOBA_TOPK):
            top = jnp.max(gate, axis=0, keepdims=True)
            first = jnp.min(jnp.where(gate == top, blk, n_blocks), axis=0, keepdims=True)
            hit = blk == first
            bias = jnp.where(hit, 0.0, bias)
            gate = jnp.where(hit, -jnp.inf, gate)
        bias_ref[h] = jnp.where(blk < i, bias, NEG_BIG).astype(_BF)

    kbar_rows = []
    for h in range(N_HEADS):
        lo = h * HEAD_DIM
        kh = _rms(zk[:, lo:lo + HEAD_DIM]) * kn_ref[...]
        k_ref[:, lo:lo + HEAD_DIM] = kh.astype(_BF)
        kbar_rows.append(jnp.mean(kh, axis=0, keepdims=True))
    kbar_all = kbar_ref[...]
    blk_row = lax.broadcasted_iota(jnp.int32, kbar_all.shape, 0)
    kbar_ref[...] = jnp.where(blk_row == i, jnp.concatenate(kbar_rows, axis=1), kbar_all)

    bg = zc[:, :D_MODEL]
    c = zc[:, D_MODEL:2 * D_MODEL] * zc[:, 2 * D_MODEL:]
    y = _causal_conv3(c, halo_ref[...], cw_ref[...])
    halo_ref[...] = c[PROJ_ROWS - CONV_HALO_ROWS:, :]
    branch_a = _dot((bg * y).astype(_BF), woa_ref[...])

    vt = _dot_nt(wvt_ref[...], xb)

    sa_ref[...] = jax.nn.sigmoid(zg[:, :D_MODEL]) * branch_a
    sgb_ref[...] = jax.nn.sigmoid(zg[:, D_MODEL:])
    for h in range(N_HEADS):
        vt_ref[h, 0] = vt[h * HEAD_DIM:(h + 1) * HEAD_DIM, :].astype(_BF)

    wup_ref[...] = wup32_ref[...].astype(_BF)
    wob_ref[...] = wob32_ref[...].astype(_BF)
    wo_ref[...] = wo32_ref[...].astype(_BF)

    @pl.when(i < WDN_CAST_STEPS)
    def _():
        wdn_ref[...] = wdn32_ref[...].astype(_BF)


def _proj(x, nm, wb, wqt, wvt, cw, qn, kn, woa, w_up, w_out_b, w_o, w_down):
    s = x.shape[0]
    assert PROJ_ROWS == MOBA_BLOCK
    nb = s // PROJ_ROWS
    cast_rows = D_MODEL // nb
    wdn_rows = D_FF // WDN_CAST_STEPS
    assert cast_rows * nb == D_MODEL and cast_rows % 16 == 0 and wdn_rows % 16 == 0 and nb >= WDN_CAST_STEPS
    const = lambda shape: pl.BlockSpec(shape, lambda i: (0,) * len(shape), pipeline_mode=pl.Buffered(1))
    w_in_cols = lambda width, blk: pl.BlockSpec((D_MODEL, width), lambda i: (0, blk),
                                                pipeline_mode=pl.Buffered(1))
    rows = lambda w: pl.BlockSpec((PROJ_ROWS, w), lambda i: (i, 0))
    slab = lambda w: pl.BlockSpec((cast_rows, w), lambda i: (i, 0))
    wdn_slab = pl.BlockSpec((wdn_rows, D_MODEL), lambda i: (jnp.minimum(i, WDN_CAST_STEPS - 1), 0))
    return pl.pallas_call(
        _proj_kernel,
        grid=(nb,),
        in_specs=[
            rows(D_MODEL),
            const((1, D_MODEL)),
            w_in_cols(3 * D_MODEL, 0),
            const((D_MODEL, D_MODEL)),
            w_in_cols(D_MODEL, 4),
            const((D_MODEL, D_MODEL)),
            w_in_cols(2 * D_MODEL, 3),
            const((3, D_MODEL)),
            const((HEAD_DIM, 1)),
            const((1, HEAD_DIM)),
            const((D_MODEL, D_MODEL)),
            slab(2 * D_FF),
            slab(D_MODEL),
            slab(D_MODEL),
            wdn_slab,
        ],
        out_specs=[
            rows(D_MODEL),
            rows(D_MODEL),
            pl.BlockSpec((D_MODEL, PROJ_ROWS), lambda i: (0, i)),
            rows(D_MODEL),
            pl.BlockSpec((N_HEADS, 1, HEAD_DIM, MOBA_BLOCK), lambda i: (0, i, 0, 0)),
            pl.BlockSpec((N_HEADS, nb, PROJ_ROWS), lambda i: (0, 0, i)),
            slab(2 * D_FF),
            slab(D_MODEL),
            slab(D_MODEL),
            wdn_slab,
        ],
        out_shape=[
            jax.ShapeDtypeStruct((s, D_MODEL), _F32),
            jax.ShapeDtypeStruct((s, D_MODEL), _F32),
            jax.ShapeDtypeStruct((D_MODEL, s), _BF),
            jax.ShapeDtypeStruct((s, D_MODEL), _BF),
            jax.ShapeDtypeStruct((N_HEADS, nb, HEAD_DIM, MOBA_BLOCK), _BF),
            jax.ShapeDtypeStruct((N_HEADS, nb, s), _BF),
            jax.ShapeDtypeStruct((D_MODEL, 2 * D_FF), _BF),
            jax.ShapeDtypeStruct((D_MODEL, D_MODEL), _BF),
            jax.ShapeDtypeStruct((D_MODEL, D_MODEL), _BF),
            jax.ShapeDtypeStruct((D_FF, D_MODEL), _BF),
        ],
        scratch_shapes=[
            pltpu.VMEM((CONV_HALO_ROWS, D_MODEL), _F32),
            pltpu.VMEM((nb, D_MODEL), _F32),
        ],
        compiler_params=pltpu.CompilerParams(
            dimension_semantics=("arbitrary",), vmem_limit_bytes=VMEM_LIMIT),
        name="proj",
    )(x, nm, wb, wqt, wb, wvt, wb, cw, qn, kn, woa, w_up, w_out_b, w_o, w_down)


def _attn_kernel(qt_ref, kblk_ref, vtblk_ref, bias_ref, o_ref, k_ref, vt_ref, qaug_ref, sa_ref, sb_ref,
                 mxa_ref, mxb_ref, m_ref, l_ref, acc_ref, *, n_blocks):
    a = pl.program_id(1)
    own0 = Q_BLOCKS * a
    own1 = own0 + 1
    for b in range(Q_BLOCKS):
        k_ref[own0 + b] = kblk_ref[b]
        for hh in range(ATTN_HEADS):
            vt_ref[hh, own0 + b] = vtblk_ref[hh, b]
    ones_rows = jnp.ones((ONES_ROWS, MOBA_BLOCK), _BF)
    pad_rows = jnp.zeros((QK_DEPTH - HEAD_DIM - n_blocks, Q_TILE), _BF)
    lane = lax.broadcasted_iota(jnp.int32, (ONES_ROWS, HEAD_DIM), 1)

    def head_rows(hh):
        return slice(hh * HEAD_DIM, (hh + 1) * HEAD_DIM)

    def unit_blocks(u):
        js = [UNIT_BLOCKS * (u - 1) + b for b in range(UNIT_BLOCKS)]
        return [jnp.where(j < own0, j, own1) for j in js]

    def keys_with_onehot(j, hh, bias_row):
        row = (lane == bias_row).astype(_BF)
        onehot = jnp.concatenate([row] * (MOBA_BLOCK // ONES_ROWS), axis=0)
        return jnp.concatenate([k_ref[j, :, head_rows(hh)], onehot], axis=1)

    def put_scores(hh, s, dst_ref, mx_ref):
        dst_ref[hh] = s
        mx_ref[hh] = jnp.max(s, axis=0, keepdims=True)

    def scores(u, hh, dst_ref, mx_ref):
        ka = jnp.concatenate([keys_with_onehot(j, hh, j) for j in unit_blocks(u)], axis=0)
        put_scores(hh, _dot(ka, qaug_ref[hh]), dst_ref, mx_ref)

    def first_scores(hh, dst_ref, mx_ref):
        ka = jnp.concatenate([keys_with_onehot(own0, hh, n_blocks), keys_with_onehot(own1, hh, n_blocks)], axis=0)
        s = _dot(ka, qaug_ref[hh])
        kpos = lax.broadcasted_iota(jnp.int32, (MOBA_BLOCK, Q_TILE), 0)
        qcol = lax.broadcasted_iota(jnp.int32, (1, Q_TILE), 1)
        bias = bias_ref[hh].astype(_F32)
        blk = lax.broadcasted_iota(jnp.int32, bias.shape, 0)
        picked = jnp.max(jnp.where(blk == own0, bias, NEG_BIG), axis=0, keepdims=True) >= 0.0
        last0 = jnp.where(qcol < MOBA_BLOCK, qcol, jnp.where(picked, MOBA_BLOCK, -1))
        last1 = qcol - MOBA_BLOCK
        s = jnp.concatenate([jnp.where(kpos <= last0, s[:MOBA_BLOCK], NEG_BIG),
                             jnp.where(kpos <= last1, s[MOBA_BLOCK:], NEG_BIG)], axis=0)
        put_scores(hh, s, dst_ref, mx_ref)

    def update(blocks, hh, src_ref, mx_ref):
        m_old = m_ref[hh]
        m_new = jnp.maximum(m_old, mx_ref[hh])
        alpha = jnp.exp2(m_old - m_new)
        outs = []
        for b, j in enumerate(blocks):
            p = jnp.exp2(src_ref[hh, b * MOBA_BLOCK:(b + 1) * MOBA_BLOCK, :] - m_new).astype(_BF)
            outs.append(_dot(jnp.concatenate([vt_ref[hh, j], ones_rows], axis=0), p))
        out = functools.reduce(jnp.add, outs)
        m_ref[hh] = m_new
        l_ref[hh] = alpha * l_ref[hh] + out[HEAD_DIM:HEAD_DIM + 1]
        acc_ref[hh] = alpha * acc_ref[hh] + out[:HEAD_DIM]

    for hh in range(ATTN_HEADS):
        qaug_ref[hh] = jnp.concatenate([qt_ref[head_rows(hh), :], bias_ref[hh], pad_rows], axis=0)
        m_ref[hh] = jnp.full(m_ref.shape[1:], NEG_BIG, _F32)
        l_ref[hh] = jnp.zeros(l_ref.shape[1:], _F32)
        acc_ref[hh] = jnp.zeros(acc_ref.shape[1:], _F32)
    for hh in range(ATTN_HEADS):
        first_scores(hh, sa_ref, mxa_ref)

    def unit_pairs(u0, n_pairs, first=False):
        for pair in range(n_pairs):
            u = u0 + 2 * pair
            for hh in range(ATTN_HEADS):
                scores(u + 1, hh, sb_ref, mxb_ref)
                update([own0, own1] if first and pair == 0 else unit_blocks(u), hh, sa_ref, mxa_ref)
            for hh in range(ATTN_HEADS):
                scores(u + 2, hh, sa_ref, mxa_ref)
                update(unit_blocks(u + 1), hh, sb_ref, mxb_ref)

    unit_pairs(0, 1, first=True)
    left = jnp.maximum(a - 1, 0)
    more_pairs = lax.shift_right_logical(left, 1)
    long_trips = lax.shift_right_logical(more_pairs, LONG_TRIP_SHIFT)

    def long_body(t, carry):
        unit_pairs(2 + 2 * LONG_TRIP_PAIRS * t, LONG_TRIP_PAIRS)
        return carry

    pairs_done = LONG_TRIP_PAIRS * long_trips
    mid_trips = lax.shift_right_logical(more_pairs - pairs_done, 1)

    def mid_body(t, carry):
        unit_pairs(2 + 2 * pairs_done, 2)
        return carry

    def short_body(t, carry):
        unit_pairs(2 + 2 * (pairs_done + 2 * mid_trips), 1)
        return carry

    def last_unit_body(t, carry):
        for hh in range(ATTN_HEADS):
            update(unit_blocks(2 + 2 * more_pairs), hh, sa_ref, mxa_ref)
        return carry

    lax.fori_loop(0, long_trips, long_body, 0)
    lax.fori_loop(0, mid_trips, mid_body, 0)
    lax.fori_loop(0, more_pairs - pairs_done - 2 * mid_trips, short_body, 0)
    lax.fori_loop(0, left - 2 * more_pairs, last_unit_body, 0)
    for hh in range(ATTN_HEADS):
        o_ref[:, head_rows(hh)] = (acc_ref[hh] * (1.0 / l_ref[hh])).astype(o_ref.dtype).T


def _attn(qt, k, vt, bias):
    s = k.shape[0]
    nb = s // MOBA_BLOCK
    assert nb < QK_DEPTH - HEAD_DIM and nb % Q_BLOCKS == 0 and Q_BLOCKS == 2
    k3 = k.reshape(nb, MOBA_BLOCK, D_MODEL)
    gw = ATTN_HEADS * HEAD_DIM
    scores_buf = pltpu.VMEM((ATTN_HEADS, UNIT_KEYS, Q_TILE), _F32)
    row_buf = pltpu.VMEM((ATTN_HEADS, 1, Q_TILE), _F32)
    return pl.pallas_call(
        functools.partial(_attn_kernel, n_blocks=nb),
        grid=(N_HEADS // ATTN_HEADS, nb // Q_BLOCKS),
        in_specs=[
            pl.BlockSpec((gw, Q_TILE), lambda g, a: (g, a)),
            pl.BlockSpec((Q_BLOCKS, MOBA_BLOCK, gw), lambda g, a: (a, 0, g)),
            pl.BlockSpec((ATTN_HEADS, Q_BLOCKS, HEAD_DIM, MOBA_BLOCK), lambda g, a: (g, a, 0, 0)),
            pl.BlockSpec((ATTN_HEADS, nb, Q_TILE), lambda g, a: (g, 0, a)),
        ],
        out_specs=pl.BlockSpec((Q_TILE, gw), lambda g, a: (a, g)),
        out_shape=jax.ShapeDtypeStruct((s, D_MODEL), _BF),
        scratch_shapes=[
            pltpu.VMEM((nb, MOBA_BLOCK, gw), _BF),
            pltpu.VMEM((ATTN_HEADS, nb, HEAD_DIM, MOBA_BLOCK), _BF),
            pltpu.VMEM((ATTN_HEADS, QK_DEPTH, Q_TILE), _BF),
            scores_buf,
            scores_buf,
            row_buf,
            row_buf,
            row_buf,
            row_buf,
            pltpu.VMEM((ATTN_HEADS, HEAD_DIM, Q_TILE), _F32),
        ],
        compiler_params=pltpu.CompilerParams(
            dimension_semantics=("arbitrary", "arbitrary"), vmem_limit_bytes=VMEM_LIMIT),
        name="moba_attn",
    )(qt, k3, vt, bias)


def _post_kernel(o_ref, sa_ref, sgb_ref, x_ref, wob_ref, wo_ref, nf_ref, wup_ref, fcw_ref, fcb_ref,
                 wdn_ref, out_ref, halo_ref, ubuf_ref):
    i = pl.program_id(0)

    @pl.when(i == 0)
    def _():
        halo_ref[...] = jnp.zeros_like(halo_ref)

    branch_b = _dot(o_ref[...], wob_ref[...])
    merged = sa_ref[...] + sgb_ref[...] * branch_b
    h = x_ref[...] + _dot(merged.astype(_BF), wo_ref[...])
    hn = (_rms(h) * nf_ref[...]).astype(_BF)

    n_chunks = D_FF // FFN_CHUNK

    def up_proj(c):
        for part in range(2):
            lo = part * D_FF + c * FFN_CHUNK
            u = _dot(hn, wup_ref[:, lo:lo + FFN_CHUNK])
            buf = ubuf_ref.at[c % (FFN_LOOKAHEAD + 1), part]
            buf[:CONV_HALO_ROWS, :] = halo_ref[:, lo:lo + FFN_CHUNK]
            buf[CONV_HALO_ROWS:, :] = u
            halo_ref[:, lo:lo + FFN_CHUNK] = u[POST_ROWS - CONV_HALO_ROWS:, :]

    def conv(c, part, r0):
        lo = part * D_FF + c * FFN_CHUNK
        buf = ubuf_ref.at[c % (FFN_LOOKAHEAD + 1), part]
        w = fcw_ref[:, lo:lo + FFN_CHUNK]
        first = CONV_HALO_ROWS - 2 + r0
        taps = [buf[first + j:first + j + ACT_ROWS, :] for j in range(3)]
        return (w[0:1, :] * taps[0] + w[1:2, :] * taps[1] + w[2:3, :] * taps[2]
                + fcb_ref[:, lo:lo + FFN_CHUNK])

    def activation(c, r0):
        half_g = 0.5 * conv(c, 0, r0)
        return ((half_g + half_g * jnp.tanh(half_g)) * conv(c, 1, r0)).astype(_BF)

    acc = h
    for c in range(min(FFN_LOOKAHEAD, n_chunks)):
        up_proj(c)
    for c in range(n_chunks):
        if c + FFN_LOOKAHEAD < n_chunks:
            up_proj(c + FFN_LOOKAHEAD)
        act = jnp.concatenate([activation(c, r0) for r0 in range(0, POST_ROWS, ACT_ROWS)], axis=0)
        acc = acc + _dot(act, wdn_ref[c * FFN_CHUNK:(c + 1) * FFN_CHUNK, :])
    out_ref[...] = acc


def _post(o, sa, sgb, x, wob, wo, nf, wup, fcw, fcb, wdn):
    s = x.shape[0]
    const = lambda shape: pl.BlockSpec(shape, lambda i: (0,) * len(shape), pipeline_mode=pl.Buffered(1))
    rows = pl.BlockSpec((POST_ROWS, D_MODEL), lambda i: (i, 0))
    return pl.pallas_call(
        _post_kernel,
        grid=(s // POST_ROWS,),
        in_specs=[
            rows, rows, rows, rows,
            const((D_MODEL, D_MODEL)),
            const((D_MODEL, D_MODEL)),
            const((1, D_MODEL)),
            const((D_MODEL, 2 * D_FF)),
            const((3, 2 * D_FF)),
            const((1, 2 * D_FF)),
            const((D_FF, D_MODEL)),
        ],
        out_specs=rows,
        out_shape=jax.ShapeDtypeStruct((s, D_MODEL), _F32),
        scratch_shapes=[
            pltpu.VMEM((CONV_HALO_ROWS, 2 * D_FF), _F32),
            pltpu.VMEM((FFN_LOOKAHEAD + 1, 2, CONV_HALO_ROWS + POST_ROWS, FFN_CHUNK), _F32),
        ],
        compiler_params=pltpu.CompilerParams(
            dimension_semantics=("arbitrary",), vmem_limit_bytes=VMEM_LIMIT),
        name="post",
    )(o, sa, sgb, x, wob, wo, nf, wup, fcw, fcb, wdn)


def kernel(x, norm_mix, w_in, conv_a_w, q_norm, k_norm, w_out_a, w_out_b, w_o, norm_ffn, w_up,
           ffn_conv_w, ffn_conv_b, w_down):
    b, s, d = x.shape
    assert b == 1 and d == D_MODEL and s % POST_ROWS == 0 and s % MOBA_BLOCK == 0
    x2 = x.reshape(s, d)
    wb, wqt, wvt, woa = _prep(w_in, w_out_a)

    sa, sgb, qt, k, vt, bias, wup, wob, wo, wdn = _proj(
        x2, norm_mix.reshape(1, d), wb, wqt, wvt, conv_a_w, q_norm.reshape(HEAD_DIM, 1),
        k_norm.reshape(1, HEAD_DIM), woa, w_up, w_out_b, w_o, w_down)
    o = _attn(qt, k, vt, bias)
    out = _post(o, sa, sgb, x2, wob, wo, norm_ffn.reshape(1, d), wup, ffn_conv_w,
                ffn_conv_b.reshape(1, 2 * D_FF), wdn)
    return out.reshape(b, s, d)
```
